```python
import jax, jax.numpy as jnp
from jax import lax
import numpy as np

D_MODEL = 1024
BATCH = 8
SEQ = 2048
DEPTH = 4
DEC_BATCH = 32
DEC_SEQ = 8
PAST_LEN = 8192
PAGE_SIZE = 128

RET_HEADS = 4
RET_DK = 128
RET_DV = 128
RET_W = RET_HEADS * RET_DV
RET_CHUNK = 128
ROPE_BASE = 10000.0
ATT_HEADS = 8
ATT_HD = 64
ATT_W = ATT_HEADS * ATT_HD
MIX_W = RET_W + ATT_W
IDX_HEADS = 8
IDX_DIM = 64
TOPK_MAX = 256
Q_BLOCK = 128
N_EXPERTS = 16
N_GROUPS = 4
EXPERTS_PER_GROUP = N_EXPERTS // N_GROUPS
TOP_K_EXPERTS = 2
D_FF_EXPERT = 512
LN_EPS = 1e-5
ALPHA = (2 * DEPTH) ** 0.25
BETA = (8 * DEPTH) ** -0.25
PROJ_SIZES = (RET_HEADS * RET_DK, RET_HEADS * RET_DK, RET_W, RET_W,
              ATT_W, ATT_W, ATT_W, IDX_HEADS * IDX_DIM, IDX_DIM, IDX_HEADS)
PROJ_W = sum(PROJ_SIZES)
F32 = jnp.float32

kernel_name = 'retnet_dsa_hymba_moe_step'


def layer_norm(x, g, b):
    xf = x.astype(F32)
    mu = jnp.mean(xf, axis=-1, keepdims=True)
    var = jnp.mean(jnp.square(xf - mu), axis=-1, keepdims=True)
    return ((xf - mu) * lax.rsqrt(var + LN_EPS) * g.astype(F32) + b.astype(F32)).astype(x.dtype)


def head_norm(o):
    mu = jnp.mean(o, axis=-1, keepdims=True)
    var = jnp.mean(jnp.square(o - mu), axis=-1, keepdims=True)
    return (o - mu) * lax.rsqrt(var + LN_EPS)


def rotary(x, pos):
    half = x.shape[-1] // 2
    inv = ROPE_BASE ** (-jnp.arange(half, dtype=F32) / half)
    ang = pos.astype(F32)[:, None] * inv[None, :]
    cos = jnp.cos(ang)[:, None, :]
    sin = jnp.sin(ang)[:, None, :]
    xf = x.astype(F32)
    x1, x2 = xf[..., :half], xf[..., half:]
    return jnp.concatenate([x1 * cos - x2 * sin, x1 * sin + x2 * cos], axis=-1).astype(x.dtype)


def ret_log_gamma():
    return jnp.log1p(-jnp.exp2(-5.0 - jnp.arange(RET_HEADS, dtype=F32)))


def project(h, pos, w_in_l):
    B, T, _ = h.shape
    p = jnp.einsum('btd,dp->btp', h, w_in_l)
    offs = [int(o) for o in np.cumsum(PROJ_SIZES)[:-1]]
    q_r, k_r, v_r, g_r, q_a, k_a, v_a, q_i, k_i, w_i = jnp.split(p, offs, axis=-1)
    q_r = rotary(q_r.reshape(B, T, RET_HEADS, RET_DK), pos)
    k_r = rotary(k_r.reshape(B, T, RET_HEADS, RET_DK), pos) * (RET_DK ** -0.5)
    v_r = v_r.reshape(B, T, RET_HEADS, RET_DV)
    q_a = q_a.reshape(B, T, ATT_HEADS, ATT_HD)
    k_a = k_a.reshape(B, T, ATT_HEADS, ATT_HD)
    v_a = v_a.reshape(B, T, ATT_HEADS, ATT_HD)
    q_i = q_i.reshape(B, T, IDX_HEADS, IDX_DIM)
    return (q_r, k_r, v_r, g_r, q_a, k_a, v_a, q_i, k_i, w_i)


def retention_chunk(q, k, v, state):
    C = q.shape[1]
    lg = ret_log_gamma()
    i = jnp.arange(C, dtype=F32)
    diff = i[:, None] - i[None, :]
    decay = jnp.where(diff >= 0, jnp.exp(lg[:, None, None] * jnp.maximum(diff, 0.0)), 0.0)
    scores = jnp.einsum('bihd,bjhd->bhij', q, k, preferred_element_type=F32) * decay
    inner = jnp.einsum('bhij,bjhe->bihe', scores, v.astype(F32))
    q_decay = jnp.exp(lg[None, :] * (i[:, None] + 1.0))
    cross = jnp.einsum('bihd,bhde->bihe', q.astype(F32), state) * q_decay[None, :, :, None]
    k_decay = jnp.exp(lg[None, :] * (C - 1.0 - i)[:, None])
    new_state = state * jnp.exp(lg * C)[None, :, None, None] + jnp.einsum(
        'bjhd,bjhe->bhde', k.astype(F32) * k_decay[None, :, :, None], v.astype(F32))
    return inner + cross, new_state


def retention_prompt(q, k, v):
    B, T = q.shape[:2]
    c = min(RET_CHUNK, T)
    nc = T // c

    def to_chunks(a):
        return a.reshape(B, nc, c, *a.shape[2:]).swapaxes(0, 1)

    def step(state, xs):
        o, state = retention_chunk(xs[0], xs[1], xs[2], state)
        return state, o

    state0 = jnp.zeros((B, RET_HEADS, RET_DK, RET_DV), F32)
    state, o = lax.scan(step, state0, (to_chunks(q), to_chunks(k), to_chunks(v)))
    return o.swapaxes(0, 1).reshape(B, T, RET_HEADS, RET_DV), state


def indexer_scores(qi, ki, wi):
    dots = jnp.einsum('...thd,...sd->...ths', qi, ki, preferred_element_type=F32) * (IDX_DIM ** -0.5)
    return jnp.einsum('...th,...ths->...ts', wi.astype(F32) * (IDX_HEADS ** -0.5), jax.nn.relu(dots))


def sparse_attend(q, k_sel, v_sel, valid):
    s = jnp.einsum('...thd,...tkhd->...thk', q, k_sel, preferred_element_type=F32) * (ATT_HD ** -0.5)
    s = jnp.where(valid[..., :, None, :], s, -jnp.inf)
    p = jax.nn.softmax(s, axis=-1)
    return jnp.einsum('...thk,...tkhd->...thd', p, v_sel.astype(F32))


def dsa_prompt(q, k, v, qi, ki, wi):
    B, T = q.shape[:2]
    ksel = min(TOPK_MAX, T // 4)
    qb = min(Q_BLOCK, T)
    nb = T // qb
    key_pos = jnp.arange(T)

    def one_block(args):
        b, blk = args
        start = blk * qb
        qpos = start + jnp.arange(qb)

        def sl(a):
            return lax.dynamic_slice_in_dim(a[b], start, qb, axis=0)

        scores = indexer_scores(sl(qi), ki[b], sl(wi))
        scores = jnp.where(key_pos[None, :] <= qpos[:, None], scores, -jnp.inf)
        _, idx = lax.top_k(scores, ksel)
        valid = idx <= qpos[:, None]
        kb, vb = k[b], v[b]
        return sparse_attend(sl(q), kb[idx], vb[idx], valid)

    b_ids = jnp.repeat(jnp.arange(B), nb)
    blk_ids = jnp.tile(jnp.arange(nb), B)
    out = lax.map(one_block, (b_ids, blk_ids))
    return out.reshape(B, T, ATT_HEADS, ATT_HD)


def dsa_sample(q, k_new, v_new, qi, ki_new, wi, cache_k_l, cache_v_l, cache_ki_l, page_table):
    DB, T = q.shape[:2]
    n_pages = page_table.shape[1]
    past = n_pages * PAGE_SIZE
    L = past + T
    ksel = min(TOPK_MAX, L // 4)
    ki_past = cache_ki_l[page_table].reshape(DB, past, IDX_DIM)
    ki_all = jnp.concatenate([ki_past, ki_new.astype(ki_past.dtype)], axis=1)
    scores = indexer_scores(qi, ki_all, wi)
    qpos = past + jnp.arange(T)
    key_pos = jnp.arange(L)
    scores = jnp.where(key_pos[None, None, :] <= qpos[None, :, None], scores, -jnp.inf)
    _, idx = lax.top_k(scores, ksel)
    valid = idx <= qpos[None, :, None]
    in_past = idx < past
    pidx = jnp.minimum(idx, past - 1)
    phys = jnp.take_along_axis(page_table, (pidx // PAGE_SIZE).reshape(DB, -1), axis=1).reshape(pidx.shape)
    off = pidx % PAGE_SIZE
    nidx = jnp.clip(idx - past, 0, T - 1)
    bi = jnp.arange(DB)[:, None, None]
    sel_past = in_past[..., None, None]
    k_sel = jnp.where(sel_past, cache_k_l[phys, off], k_new[bi, nidx].astype(cache_k_l.dtype))
    v_sel = jnp.where(sel_past, cache_v_l[phys, off], v_new[bi, nidx].astype(cache_v_l.dtype))
    return sparse_attend(q, k_sel, v_sel, valid)


def merge_heads(o_ret, g_ret, o_att, gn_w, w_out_l, dtype):
    B, T = o_ret.shape[:2]
    r = head_norm(o_ret).reshape(B, T, RET_W) * gn_w.astype(F32)
    r = jax.nn.silu(g_ret.astype(F32)) * r
    cat = jnp.concatenate([r, o_att.reshape(B, T, ATT_W).astype(F32)], axis=-1).astype(dtype)
    return jnp.einsum('btm,md->btd', cat, w_out_l)


def route(h, router_w, router_b):
    N = h.shape[0]
    aff = jax.nn.sigmoid(jnp.einsum('nd,de->ne', h, router_w, preferred_element_type=F32))
    biased = aff + router_b.astype(F32)
    grp = biased.reshape(N, N_GROUPS, EXPERTS_PER_GROUP)
    grp_score = jnp.sum(lax.top_k(grp, 2)[0], axis=-1)
    _, gsel = lax.top_k(grp_score, 1)
    emask = jnp.repeat(jax.nn.one_hot(gsel[:, 0], N_GROUPS, dtype=F32), EXPERTS_PER_GROUP, axis=-1)
    masked = jnp.where(emask > 0, biased, -jnp.inf)
    _, eidx = lax.top_k(masked, TOP_K_EXPERTS)
    sel = jnp.take_along_axis(aff, eidx, axis=-1)
    wts = sel / jnp.sum(sel, axis=-1, keepdims=True)
    return jnp.sum(jax.nn.one_hot(eidx, N_EXPERTS, dtype=F32) * wts[..., None], axis=-2)


def moe(h, gates, wg, wu, wd):
    out = jnp.zeros(h.shape, F32)
    for e in range(N_EXPERTS):
        a = jax.nn.silu(h @ wg[e]) * (h @ wu[e])
        out = out + gates[:, e:e + 1] * (a @ wd[e]).astype(F32)
    return out.astype(h.dtype)


def channel_block(x, mix, ln1_g, ln1_b, router_w, router_b, wg, wu, wd, ln2_g, ln2_b):
    x = layer_norm(ALPHA * x + mix, ln1_g, ln1_b)
    B, T, D = x.shape
    h = x.reshape(B * T, D)
    f = moe(h, route(h, router_w, router_b), wg, wu, wd).reshape(B, T, D)
    return layer_norm(ALPHA * x + f, ln2_g, ln2_b)


def setup_inputs(seed: int = 0) -> dict:
    key = jax.random.key(seed)
    ks = jax.random.split(key, 24)
    n_pages = PAST_LEN // PAGE_SIZE
    n_used = DEC_BATCH * n_pages
    n_pool = n_used + max(1, n_used // 4)
    perm = jax.random.permutation(ks[0], n_pool)
    page_table = perm[:n_used].reshape(DEC_BATCH, n_pages).astype(jnp.int32)
    nrm = jax.random.normal
    return {
        'x_prompt': nrm(ks[1], (BATCH, SEQ, D_MODEL), F32),
        'x_sample': nrm(ks[2], (DEC_BATCH, DEC_SEQ, D_MODEL), F32),
        'cache_k': nrm(ks[3], (DEPTH, n_pool, PAGE_SIZE, ATT_HEADS, ATT_HD), F32),
        'cache_v': nrm(ks[4], (DEPTH, n_pool, PAGE_SIZE, ATT_HEADS, ATT_HD), F32),
        'cache_kidx': nrm(ks[5], (DEPTH, n_pool, PAGE_SIZE, IDX_DIM), F32),
        'state_ret': 0.5 * nrm(ks[6], (DEPTH, DEC_BATCH, RET_HEADS, RET_DK, RET_DV), F32),
        'page_table': page_table,
        'w_in': nrm(ks[7], (DEPTH, D_MODEL, PROJ_W), F32) * D_MODEL ** -0.5,
        'ret_gn_w': 1.0 + 0.02 * nrm(ks[8], (DEPTH, RET_W), F32),
        'w_out': nrm(ks[9], (DEPTH, MIX_W, D_MODEL), F32) * (MIX_W ** -0.5 * BETA),
        'ln1_g': 1.0 + 0.02 * nrm(ks[10], (DEPTH, D_MODEL), F32),
        'ln1_b': 0.02 * nrm(ks[11], (DEPTH, D_MODEL), F32),
        'router_w': nrm(ks[12], (D_MODEL, N_EXPERTS), F32) * D_MODEL ** -0.5,
        'router_b': 0.01 * nrm(ks[13], (N_EXPERTS,), F32),
        'w_gate': nrm(ks[14], (DEPTH, N_EXPERTS, D_MODEL, D_FF_EXPERT), F32) * D_MODEL ** -0.5,
        'w_up': nrm(ks[15], (DEPTH, N_EXPERTS, D_MODEL, D_FF_EXPERT), F32) * D_MODEL ** -0.5,
        'w_down': nrm(ks[16], (DEPTH, N_EXPERTS, D_FF_EXPERT, D_MODEL), F32) * (D_FF_EXPERT ** -0.5 * BETA),
        'ln2_g': 1.0 + 0.02 * nrm(ks[17], (DEPTH, D_MODEL), F32),
        'ln2_b': 0.02 * nrm(ks[18], (DEPTH, D_MODEL), F32),
    }


def reference(x_prompt, x_sample, cache_k, cache_v, cache_kidx, state_ret, page_table,
              w_in, ret_gn_w, w_out, ln1_g, ln1_b, router_w, router_b,
              w_gate, w_up, w_down, ln2_g, ln2_b):
    past = page_table.shape[1] * PAGE_SIZE
    pos_p = jnp.arange(x_prompt.shape[1])
    pos_s = past + jnp.arange(x_sample.shape[1])
    xp, xs = x_prompt, x_sample
    kp_l, vp_l, kip_l, rp_l = [], [], [], []
    ks_l, vs_l, kis_l, rs_l = [], [], [], []
    for l in range(DEPTH):
        q_r, k_r, v_r, g_r, q_a, k_a, v_a, q_i, k_i, w_i = project(xp, pos_p, w_in[l])
        o_r, st = retention_prompt(q_r, k_r, v_r)
        o_a = dsa_prompt(q_a, k_a, v_a, q_i, k_i, w_i)
        mix = merge_heads(o_r, g_r, o_a, ret_gn_w[l], w_out[l], xp.dtype)
        xp = channel_block(xp, mix, ln1_g[l], ln1_b[l], router_w, router_b,
                           w_gate[l], w_up[l], w_down[l], ln2_g[l], ln2_b[l])
        kp_l.append(k_a)
        vp_l.append(v_a)
        kip_l.append(k_i)
        rp_l.append(st.astype(state_ret.dtype))
        q_r, k_r, v_r, g_r, q_a, k_a, v_a, q_i, k_i, w_i = project(xs, pos_s, w_in[l])
        o_r, st = retention_chunk(q_r, k_r, v_r, state_ret[l].astype(F32))
        o_a = dsa_sample(q_a, k_a, v_a, q_i, k_i, w_i, cache_k[l], cache_v[l], cache_kidx[l], page_table)
        mix = merge_heads(o_r, g_r, o_a, ret_gn_w[l], w_out[l], xs.dtype)
        xs = channel_block(xs, mix, ln1_g[l], ln1_b[l], router_w, router_b,
                           w_gate[l], w_up[l], w_down[l], ln2_g[l], ln2_b[l])
        ks_l.append(k_a)
        vs_l.append(v_a)
        kis_l.append(k_i)
        rs_l.append(st.astype(state_ret.dtype))
    k_prompt = jnp.stack(kp_l)
    v_prompt = jnp.stack(vp_l)
    kidx_prompt = jnp.stack(kip_l)
    ret_prompt = jnp.stack(rp_l)
    k_sample = jnp.stack(ks_l)
    v_sample = jnp.stack(vs_l)
    kidx_sample = jnp.stack(kis_l)
    ret_sample = jnp.stack(rs_l)
    return (xp, xs, k_prompt, v_prompt, kidx_prompt, ret_prompt, k_sample, v_sample, kidx_sample, ret_sample)
```

```python
import functools

import numpy as np
import jax
import jax.numpy as jnp
from jax import lax
from jax.experimental import pallas as pl
from jax.experimental.pallas import tpu as pltpu

F32 = jnp.float32
BF16 = jnp.bfloat16
I32 = jnp.int32

RET_HEADS = 4
RET_D = 128
ATT_HEADS = 8
ATT_HD = 64
IDX_HEADS = 8
IDX_DIM = 64
TOPK_MAX = 256
PAGE = 128
N_EXPERTS = 16
N_GROUPS = 4
GROUP_SIZE = N_EXPERTS // N_GROUPS
ROPE_BASE = 10000.0
LN_EPS = 1e-5
D_MODEL = 1024
MAIN_W = 4096
TAIL_W = 128
LANES = 128
VMEM_LIMIT = 56 * 1024 * 1024

INT_MIN = -(2 ** 31)
NEG_INF = float("-inf")


def _log_gammas():
    h = np.arange(RET_HEADS, dtype=np.float32)
    return [float(v) for v in np.log1p(-np.exp2(-5.0 - h)).astype(np.float32)]


def _cparams(sem):
    return pltpu.CompilerParams(dimension_semantics=sem, vmem_limit_bytes=VMEM_LIMIT)


def _proj_body(x_ref, wm_ref, wt_ref, pm_ref, pt_ref, xb_ref):
    @pl.when(pl.program_id(1) == 0)
    def _():
        xb = x_ref[...].astype(BF16)
        xb_ref[...] = xb
        pt_ref[...] = jnp.dot(xb, wt_ref[0].astype(BF16), preferred_element_type=F32)

    pm_ref[...] = jnp.dot(xb_ref[...], wm_ref[0].astype(BF16), preferred_element_type=F32)


def _project(x2d, w_in, w_tail, layer, tm):
    n = x2d.shape[0]
    tn = 512
    return pl.pallas_call(
        _proj_body,
        grid=(n // tm, MAIN_W // tn),
        in_specs=[
            pl.BlockSpec((tm, D_MODEL), lambda i, j: (i, 0)),
            pl.BlockSpec((1, D_MODEL, tn), lambda i, j: (layer, 0, j)),
            pl.BlockSpec((1, D_MODEL, TAIL_W), lambda i, j: (layer, 0, 0)),
        ],
        out_specs=[
            pl.BlockSpec((tm, tn), lambda i, j: (i, j)),
            pl.BlockSpec((tm, TAIL_W), lambda i, j: (i, 0)),
        ],
        out_shape=[jax.ShapeDtypeStruct((n, MAIN_W), F32), jax.ShapeDtypeStruct((n, TAIL_W), F32)],
        scratch_shapes=[pltpu.VMEM((tm, D_MODEL), BF16)],
        compiler_params=_cparams(("arbitrary", "arbitrary")),
        name="proj",
    )(x2d, w_in, w_tail)


def _ret_body(q_ref, k_ref, v_ref, cos_ref, sin_ref, s0_ref, o_ref, sout_ref, st_ref, *, C, NC):
    c = pl.program_id(1)
    CP = max(C, LANES)

    @pl.when(c == 0)
    def _():
        st_ref[...] = s0_ref[0]

    def pad(a):
        if CP == C:
            return a
        return jnp.concatenate([a, jnp.zeros((CP - C, a.shape[1]), a.dtype)], axis=0)

    cosf = pad(cos_ref[...])
    sinf = pad(sin_ref[...])
    ii = lax.broadcasted_iota(I32, (CP, CP), 0)
    jj = lax.broadcasted_iota(I32, (CP, CP), 1)
    diff = (ii - jj).astype(F32)
    pos = lax.broadcasted_iota(I32, (CP, 1), 0).astype(F32)
    for h, lg in enumerate(_log_gammas()):
        sl = slice(h * RET_D, (h + 1) * RET_D)
        q = pad(q_ref[0, :, sl])
        k = pad(k_ref[0, :, sl])
        vb = pad(v_ref[0, :, sl]).astype(BF16)
        q = q * cosf + pltpu.roll(q, RET_D // 2, 1) * sinf
        k = (k * cosf + pltpu.roll(k, RET_D // 2, 1) * sinf) * (RET_D ** -0.5)
        decay = jnp.where(diff >= 0, jnp.exp(lg * jnp.maximum(diff, 0.0)), 0.0)
        qb = q.astype(BF16)
        scores = lax.dot_general(qb, k.astype(BF16), (((1,), (1,)), ((), ())),
                                 preferred_element_type=F32) * decay
        inner = jnp.dot(scores.astype(BF16), vb, preferred_element_type=F32)
        st = st_ref[h]
        cross = jnp.dot(qb, st.astype(BF16), preferred_element_type=F32) * jnp.exp(lg * (pos + 1.0))
        o_ref[0, :, sl] = (inner + cross)[:C]
        kd = k * jnp.exp(lg * (C - 1.0 - pos))
        st_ref[h] = st * float(np.exp(np.float32(lg) * np.float32(C))) + jnp.dot(
            kd.T.astype(BF16), vb, preferred_element_type=F32)

    @pl.when(c == NC - 1)
    def _():
        sout_ref[0] = st_ref[...]


def _retention(pm3, cosf, sinf, state0, C):
    B, T, _ = pm3.shape
    NC = T // C
    W = RET_HEADS * RET_D
    return pl.pallas_call(
        functools.partial(_ret_body, C=C, NC=NC),
        grid=(B, NC),
        in_specs=[
            pl.BlockSpec((1, C, W), lambda b, c: (b, c, 0)),
            pl.BlockSpec((1, C, W), lambda b, c: (b, c, 1)),
            pl.BlockSpec((1, C, W), lambda b, c: (b, c, 2)),
            pl.BlockSpec((C, RET_D), lambda b, c: (c, 0)),
            pl.BlockSpec((C, RET_D), lambda b, c: (c, 0)),
            pl.BlockSpec((1, RET_HEADS, RET_D, RET_D), lambda b, c: (b, 0, 0, 0)),
        ],
        out_specs=[
            pl.BlockSpec((1, C, W), lambda b, c: (b, c, 0)),
            pl.BlockSpec((1, RET_HEADS, RET_D, RET_D), lambda b, c: (b, 0, 0, 0)),
        ],
        out_shape=[jax.ShapeDtypeStruct((B, T, W), F32),
                   jax.ShapeDtypeStruct((B, RET_HEADS, RET_D, RET_D), F32)],
        scratch_shapes=[pltpu.VMEM((RET_HEADS, RET_D, RET_D), F32)],
        compiler_params=_cparams(("arbitrary", "arbitrary")),
        name="retention",
    )(pm3, pm3, pm3, cosf, sinf, state0)


def _sortable(x):
    bits = pltpu.bitcast(x + 0.0, I32)
    return bits ^ ((bits >> 31) & 0x7FFFFFFF)


def _kth_largest(count_ge, shape, kf):
    c0 = count_ge(jnp.zeros(shape, I32))
    tau = jnp.where(c0 >= kf, 0, INT_MIN).astype(I32)

    def body(it, tau):
        cand = tau | (jnp.int32(1) << (30 - it))
        return jnp.where(count_ge(cand) >= kf, cand, tau)

    return lax.fori_loop(0, 31, body, tau)


def _tie_cutoff(count_lt, shape, need, nbits):
    def body(it, m):
        cand = m + (jnp.int32(1) << (nbits - 1 - it))
        return jnp.where(count_lt(cand) < need, cand, m)

    return lax.fori_loop(0, nbits, body, jnp.zeros(shape, I32))


def _dsa_prompt_body(qa_ref, k_ref, v_ref, qi_ref, pt_ref, ptq_ref, o_ref,
                     kb_ref, vt_ref, kib_ref, keys_ref, bias_ref, *, T, QB, ksel):
    i = pl.program_id(1)

    @pl.when(i == 0)
    def _():
        kb_ref[...] = k_ref[0].astype(BF16)
        vt_ref[...] = v_ref[0].T.astype(BF16)
        kib_ref[...] = pt_ref[0].astype(BF16)

    qiT = qi_ref[0].T
    ptT = ptq_ref[0].T
    zeros64 = jnp.zeros((IDX_DIM, QB), BF16)
    score = jnp.zeros((T, QB), F32)
    for h in range(IDX_HEADS):
        rhs = jnp.concatenate([qiT[h * IDX_DIM:(h + 1) * IDX_DIM].astype(BF16), zeros64], axis=0)
        d = jnp.dot(kib_ref[...], rhs, preferred_element_type=F32)
        w = ptT[IDX_DIM + h:IDX_DIM + h + 1, :] * (IDX_HEADS ** -0.5)
        score = score + jnp.maximum(d * (IDX_DIM ** -0.5), 0.0) * w

    s_pos = lax.broadcasted_iota(I32, (T, QB), 0)
    t_pos = i * QB + lax.broadcasted_iota(I32, (T, QB), 1)
    causal = s_pos <= t_pos
    keys_ref[...] = _sortable(jnp.where(causal, score, NEG_INF))

    kf = float(ksel)

    def count_ge(cand):
        return jnp.sum(jnp.where(keys_ref[...] >= cand, 1.0, 0.0), axis=0, keepdims=True)

    tau = _kth_largest(count_ge, (1, QB), kf)
    keys = keys_ref[...]
    gt = keys > tau
    eq = (keys == tau) & causal
    need = kf - jnp.sum(jnp.where(gt, 1.0, 0.0), axis=0, keepdims=True)
    n_eq = jnp.sum(jnp.where(eq, 1.0, 0.0), axis=0, keepdims=True)
    any_split = jnp.max(jnp.where(n_eq > need, 1.0, 0.0)) > 0.0

    def slow():
        def count_lt(m):
            hit = (keys_ref[...] == tau) & (s_pos <= t_pos) & (s_pos < m)
            return jnp.sum(jnp.where(hit, 1.0, 0.0), axis=0, keepdims=True)
        return _tie_cutoff(count_lt, (1, QB), need, int(T).bit_length())

    mcut = lax.cond(any_split, slow, lambda: jnp.full((1, QB), T, I32))
    sel = causal & (gt | (eq & (s_pos <= mcut)))
    bias_ref[...] = jnp.where(sel, 0.0, NEG_INF)

    qaT = qa_ref[0].T
    zq = jnp.zeros((ATT_HD, QB), BF16)
    outs = []
    for j in range(ATT_HEADS // 2):
        r0 = 2 * j * ATT_HD
        top = jnp.concatenate([qaT[r0:r0 + ATT_HD].astype(BF16), zq], axis=1)
        bot = jnp.concatenate([zq, qaT[r0 + ATT_HD:r0 + 2 * ATT_HD].astype(BF16)], axis=1)
        rhs = jnp.concatenate([top, bot], axis=0)
        sT = jnp.dot(kb_ref[:, r0:r0 + 2 * ATT_HD], rhs, preferred_element_type=F32)
        for u in range(2):
            h = 2 * j + u
            s = sT[:, u * QB:(u + 1) * QB] * (ATT_HD ** -0.5) + bias_ref[...]
            m = jnp.max(s, axis=0, keepdims=True)
            p = jnp.exp(s - m)
            l = jnp.sum(p, axis=0, keepdims=True)
            oT = jnp.dot(vt_ref[h * ATT_HD:(h + 1) * ATT_HD, :], p.astype(BF16),
                         preferred_element_type=F32)
            outs.append(oT / l)
    o_ref[0] = jnp.concatenate(outs, axis=0).T


def _dsa_prompt(pm3, pt3):
    B, T, _ = pm3.shape
    QB = min(LANES, T)
    ksel = min(TOPK_MAX, T // 4)
    W = ATT_HEADS * ATT_HD
    return pl.pallas_call(
        functools.partial(_dsa_prompt_body, T=T, QB=QB, ksel=ksel),
        grid=(B, T // QB),
        in_specs=[
            pl.BlockSpec((1, QB, W), lambda b, i: (b, i, 4)),
            pl.BlockSpec((1, T, W), lambda b, i: (b, 0, 5)),
            pl.BlockSpec((1, T, W), lambda b, i: (b, 0, 6)),
            pl.BlockSpec((1, QB, W), lambda b, i: (b, i, 7)),
            pl.BlockSpec((1, T, TAIL_W), lambda b, i: (b, 0, 0)),
            pl.BlockSpec((1, QB, TAIL_W), lambda b, i: (b, i, 0)),
        ],
        out_specs=pl.BlockSpec((1, QB, W), lambda b, i: (b, i, 0)),
        out_shape=jax.ShapeDtypeStruct((B, T, W), F32),
        scratch_shapes=[pltpu.VMEM((T, W), BF16), pltpu.VMEM((W, T), BF16),
                        pltpu.VMEM((T, TAIL_W), BF16), pltpu.VMEM((T, QB), I32),
                        pltpu.VMEM((T, QB), F32)],
        compiler_params=_cparams(("arbitrary", "arbitrary")),
        name="dsa_prompt",
    )(pm3, pm3, pm3, pm3, pt3, pt3)


PAGES_PER_STEP = 8


def _idx_sample_body(ptab_ref, *refs, P):
    del ptab_ref
    page_refs = refs[:P]
    qi_ref, w_ref, kin_ref, sc_ref, scn_ref = refs[P:]
    q = qi_ref[0].astype(BF16)
    w = w_ref[0] * (IDX_HEADS ** -0.5)
    nq = q.shape[0] // IDX_HEADS

    def scores(keys):
        d = lax.dot_general(q, keys.astype(BF16), (((1,), (1,)), ((), ())), preferred_element_type=F32)
        r = jnp.maximum(d * (IDX_DIM ** -0.5), 0.0) * w
        acc = r[0:nq]
        for h in range(1, IDX_HEADS):
            acc = acc + r[h * nq:(h + 1) * nq]
        return acc

    sc_ref[0, 0] = scores(jnp.concatenate([r[0, 0] for r in page_refs], axis=0))

    @pl.when(pl.program_id(1) == 0)
    def _():
        scn_ref[0] = scores(kin_ref[0])


def _idx_sample(page_table, cache_ki, layer, qi_flat, w_col, ki_new_pad):
    DB, n_pages = page_table.shape
    P = PAGES_PER_STEP
    NCH = n_pages // P
    nq = qi_flat.shape[1] // IDX_HEADS
    page_specs = [
        pl.BlockSpec((1, 1, PAGE, IDX_DIM), functools.partial(
            lambda b, c, pt, p: (layer, pt[b, c * P + p], 0, 0), p=p))
        for p in range(P)
    ]
    grid_spec = pltpu.PrefetchScalarGridSpec(
        num_scalar_prefetch=1,
        grid=(DB, NCH),
        in_specs=page_specs + [
            pl.BlockSpec((1, IDX_HEADS * nq, IDX_DIM), lambda b, c, pt: (b, 0, 0)),
            pl.BlockSpec((1, IDX_HEADS * nq, 1), lambda b, c, pt: (b, 0, 0)),
            pl.BlockSpec((1, PAGE, IDX_DIM), lambda b, c, pt: (b, 0, 0)),
        ],
        out_specs=[
            pl.BlockSpec((1, 1, nq, P * PAGE), lambda b, c, pt: (b, c, 0, 0)),
            pl.BlockSpec((1, nq, PAGE), lambda b, c, pt: (b, 0, 0)),
        ],
    )
    return pl.pallas_call(
        functools.partial(_idx_sample_body, P=P),
        grid_spec=grid_spec,
        out_shape=[jax.ShapeDtypeStruct((DB, NCH, nq, P * PAGE), F32),
                   jax.ShapeDtypeStruct((DB, nq, PAGE), F32)],
        compiler_params=_cparams(("arbitrary", "arbitrary")),
        name="idx_sample",
    )(page_table, *([cache_ki] * P), qi_flat, w_col, ki_new_pad)


def _select_sample_body(sc_ref, scn_ref, bm_ref, bn_ref, keys_ref, *, NCH, nq, past, ksel):
    S = past + PAGE
    main = jnp.concatenate([sc_ref[0, c] for c in range(NCH)], axis=1)
    t_new = lax.broadcasted_iota(I32, (nq, PAGE), 0)
    s_new = lax.broadcasted_iota(I32, (nq, PAGE), 1)
    new = jnp.where(s_new <= t_new, scn_ref[0], NEG_INF)
    keys_ref[...] = _sortable(jnp.concatenate([main, new], axis=1))
    pos = lax.broadcasted_iota(I32, (nq, S), 1)
    valid = (pos < past) | ((pos - past) <= lax.broadcasted_iota(I32, (nq, S), 0))
    kf = float(ksel)

    def count_ge(cand):
        return jnp.sum(jnp.where(keys_ref[...] >= cand, 1.0, 0.0), axis=1, keepdims=True)

    tau = _kth_largest(count_ge, (nq, 1), kf)
    keys = keys_ref[...]
    gt = keys > tau
    eq = (keys == tau) & valid
    need = kf - jnp.sum(jnp.where(gt, 1.0, 0.0), axis=1, keepdims=True)
    n_eq = jnp.sum(jnp.where(eq, 1.0, 0.0), axis=1, keepdims=True)
    any_split = jnp.max(jnp.where(n_eq > need, 1.0, 0.0)) > 0.0

    def slow():
        def count_lt(m):
            hit = (keys_ref[...] == tau) & valid & (pos < m)
            return jnp.sum(jnp.where(hit, 1.0, 0.0), axis=1, keepdims=True)
        return _tie_cutoff(count_lt, (nq, 1), need, int(S).bit_length())

    mcut = lax.cond(any_split, slow, lambda: jnp.full((nq, 1), S, I32))
    sel = valid & (gt | (eq & (pos <= mcut)))
    bias = jnp.where(sel, 0.0, NEG_INF)
    bm_ref[0] = bias[:, :past]
    bn_ref[0] = bias[:, past:]


def _select_sample(sc, scn, ksel):
    DB, NCH, nq, CW = sc.shape
    past = NCH * CW
    return pl.pallas_call(
        functools.partial(_select_sample_body, NCH=NCH, nq=nq, past=past, ksel=ksel),
        grid=(DB,),
        in_specs=[pl.BlockSpec((1, NCH, nq, CW), lambda b: (b, 0, 0, 0)),
                  pl.BlockSpec((1, nq, PAGE), lambda b: (b, 0, 0))],
        out_specs=[pl.BlockSpec((1, nq, past), lambda b: (b, 0, 0)),
                   pl.BlockSpec((1, nq, PAGE), lambda b: (b, 0, 0))],
        out_shape=[jax.ShapeDtypeStruct((DB, nq, past), F32),
                   jax.ShapeDtypeStruct((DB, nq, PAGE), F32)],
        scratch_shapes=[pltpu.VMEM((nq, past + PAGE), I32)],
        compiler_params=_cparams(("arbitrary",)),
        name="select_sample",
    )(sc, scn)


def _att_sample_body(ptab_ref, *refs, P, NCH, nq):
    del ptab_ref
    k_refs = refs[:P]
    v_refs = refs[P:2 * P]
    q_ref, bm_ref, bn_ref, kn_ref, vn_ref, o_ref, m_ref, l_ref, acc_ref = refs[2 * P:]
    c = pl.program_id(1)

    @pl.when(c == 0)
    def _():
        m_ref[...] = jnp.full(m_ref.shape, NEG_INF, F32)
        l_ref[...] = jnp.zeros(l_ref.shape, F32)
        acc_ref[...] = jnp.zeros(acc_ref.shape, F32)

    q = q_ref[0].astype(BF16)

    def update(k, v, bias):
        s = lax.dot_general(q, k.astype(BF16), (((1,), (1,)), ((), ())), preferred_element_type=F32)
        s = s * (ATT_HD ** -0.5) + jnp.concatenate([bias] * ATT_HEADS, axis=0)
        m_old = m_ref[...]
        m_new = jnp.maximum(m_old, jnp.max(s, axis=1, keepdims=True))
        m_safe = jnp.where(m_new == NEG_INF, 0.0, m_new)
        alpha = jnp.exp(m_old - m_safe)
        p = jnp.exp(s - m_safe)
        l_ref[...] = l_ref[...] * alpha + jnp.sum(p, axis=1, keepdims=True)
        acc_ref[...] = acc_ref[...] * alpha + jnp.dot(p.astype(BF16), v.astype(BF16),
                                                      preferred_element_type=F32)
        m_ref[...] = m_new

    update(jnp.concatenate([r[0, 0] for r in k_refs], axis=0),
           jnp.concatenate([r[0, 0] for r in v_refs], axis=0), bm_ref[0])

    @pl.when(c == NCH - 1)
    def _():
        update(kn_ref[0], vn_ref[0], bn_ref[0])
        o = acc_ref[...] / l_ref[...]
        head_of_lane = lax.broadcasted_iota(I32, (nq, ATT_HEADS * ATT_HD), 1) // ATT_HD
        out = jnp.zeros((nq, ATT_HEADS * ATT_HD), F32)
        for h in range(ATT_HEADS):
            out = out + jnp.where(head_of_lane == h, o[h * nq:(h + 1) * nq], 0.0)
        o_ref[0] = out


def _att_sample(page_table, cache_k, cache_v, layer, q_bd, bias_main, bias_new, k_new_pad, v_new_pad):
    DB, n_pages = page_table.shape
    P = PAGES_PER_STEP
    NCH = n_pages // P
    nq = bias_main.shape[1]
    W = ATT_HEADS * ATT_HD

    def page_spec(p):
        return pl.BlockSpec((1, 1, PAGE, W), lambda b, c, pt: (layer, pt[b, c * P + p], 0, 0))

    grid_spec = pltpu.PrefetchScalarGridSpec(
        num_scalar_prefetch=1,
        grid=(DB, NCH),
        in_specs=[page_spec(p) for p in range(P)] + [page_spec(p) for p in range(P)] + [
            pl.BlockSpec((1, ATT_HEADS * nq, W), lambda b, c, pt: (b, 0, 0)),
            pl.BlockSpec((1, nq, P * PAGE), lambda b, c, pt: (b, 0, c)),
            pl.BlockSpec((1, nq, PAGE), lambda b, c, pt: (b, 0, 0)),
            pl.BlockSpec((1, PAGE, W), lambda b, c, pt: (b, 0, 0)),
            pl.BlockSpec((1, PAGE, W), lambda b, c, pt: (b, 0, 0)),
        ],
        out_specs=pl.BlockSpec((1, nq, W), lambda b, c, pt: (b, 0, 0)),
        scratch_shapes=[pltpu.VMEM((ATT_HEADS * nq, 1), F32), pltpu.VMEM((ATT_HEADS * nq, 1), F32),
                        pltpu.VMEM((ATT_HEADS * nq, W), F32)],
    )
    return pl.pallas_call(
        functools.partial(_att_sample_body, P=P, NCH=NCH, nq=nq),
        grid_spec=grid_spec,
        out_shape=jax.ShapeDtypeStruct((DB, nq, W), F32),
        compiler_params=_cparams(("arbitrary", "arbitrary")),
        name="att_sample",
    )(page_table, *([cache_k] * P), *([cache_v] * P), q_bd, bias_main, bias_new, k_new_pad, v_new_pad)


def _dsa_sample(pm3, pt3, cache_k4, cache_v4, cache_ki, page_table, layer):
    DB, nq, _ = pm3.shape
    W = ATT_HEADS * ATT_HD
    past = page_table.shape[1] * PAGE
    ksel = min(TOPK_MAX, (past + nq) // 4)
    q_a = pm3[:, :, 4 * W:5 * W].reshape(DB, nq, ATT_HEADS, ATT_HD)
    q_i = pm3[:, :, 7 * W:8 * W].reshape(DB, nq, IDX_HEADS, IDX_DIM)
    qi_flat = q_i.transpose(0, 2, 1, 3).reshape(DB, IDX_HEADS * nq, IDX_DIM)
    w_col = pt3[:, :, IDX_DIM:IDX_DIM + IDX_HEADS].transpose(0, 2, 1).reshape(DB, IDX_HEADS * nq, 1)
    q_bd = jnp.einsum("bthd,hg->bhtgd", q_a, jnp.eye(ATT_HEADS, dtype=F32)).reshape(DB, ATT_HEADS * nq, W)
    rows = ((0, 0), (0, PAGE - nq), (0, 0))
    ki_new_pad = jnp.pad(pt3[:, :, :IDX_DIM], rows)
    k_new_pad = jnp.pad(pm3[:, :, 5 * W:6 * W], rows)
    v_new_pad = jnp.pad(pm3[:, :, 6 * W:7 * W], rows)
    sc, scn = _idx_sample(page_table, cache_ki, layer, qi_flat, w_col, ki_new_pad)
    bias_main, bias_new = _select_sample(sc, scn, ksel)
    return _att_sample(page_table, cache_k4, cache_v4, layer, q_bd, bias_main, bias_new, k_new_pad, v_new_pad)


def _layer_norm(y, g, b):
    mu = jnp.mean(y, axis=-1, keepdims=True)
    var = jnp.mean(jnp.square(y - mu), axis=-1, keepdims=True)
    return (y - mu) * lax.rsqrt(var + LN_EPS) * g + b


def _top2_of4(b0, b1, b2, b3):
    hi01, lo01 = jnp.maximum(b0, b1), jnp.minimum(b0, b1)
    hi23, lo23 = jnp.maximum(b2, b3), jnp.minimum(b2, b3)
    return jnp.maximum(hi01, hi23), jnp.maximum(jnp.minimum(hi01, hi23), jnp.maximum(lo01, lo23))


def _route_rows(aff, biased):
    gscore = []
    for g in range(N_GROUPS):
        m1, m2 = _top2_of4(*biased[g * GROUP_SIZE:(g + 1) * GROUP_SIZE])
        gscore.append(m1 + m2)
    gmax = functools.reduce(jnp.maximum, gscore)
    taken = jnp.zeros_like(gmax, dtype=jnp.bool_)
    gsel = []
    for g in range(N_GROUPS):
        hit = (gscore[g] == gmax) & jnp.logical_not(taken)
        gsel.append(hit)
        taken = taken | hit
    cand, caff = [], []
    for k in range(GROUP_SIZE):
        c = jnp.full_like(gmax, NEG_INF)
        a = jnp.zeros_like(gmax)
        for g in range(N_GROUPS):
            c = jnp.where(gsel[g], biased[g * GROUP_SIZE + k], c)
            a = jnp.where(gsel[g], aff[g * GROUP_SIZE + k], a)
        cand.append(c)
        caff.append(a)
    best = functools.reduce(jnp.maximum, cand)
    taken = jnp.zeros_like(taken)
    first = []
    for k in range(GROUP_SIZE):
        hit = (cand[k] == best) & jnp.logical_not(taken)
        first.append(hit)
        taken = taken | hit
    rest = [jnp.where(first[k], NEG_INF, cand[k]) for k in range(GROUP_SIZE)]
    best2 = functools.reduce(jnp.maximum, rest)
    taken = jnp.zeros_like(taken)
    second = []
    for k in range(GROUP_SIZE):
        hit = (rest[k] == best2) & jnp.logical_not(first[k]) & jnp.logical_not(taken)
        second.append(hit)
        taken = taken | hit
    a1 = functools.reduce(jnp.add, [jnp.where(first[k], caff[k], 0.0) for k in range(GROUP_SIZE)])
    a2 = functools.reduce(jnp.add, [jnp.where(second[k], caff[k], 0.0) for k in range(GROUP_SIZE)])
    tot = a1 + a2
    w1, w2 = a1 / tot, a2 / tot
    gates = []
    for g in range(N_GROUPS):
        for k in range(GROUP_SIZE):
            gates.append(jnp.where(gsel[g] & first[k], w1, jnp.where(gsel[g] & second[k], w2, 0.0)))
    return gates


def _merge_body(oret_ref, g_ref, oatt_ref, x_ref, gn_ref, wout_ref, lg_ref, lb_ref, rwt_ref, rb_ref,
                x1_ref, gates_ref, wob_ref, *, alpha):
    @pl.when(pl.program_id(0) == 0)
    def _():
        wob_ref[...] = wout_ref[0].astype(BF16)

    parts = []
    for h in range(RET_HEADS):
        sl = slice(h * RET_D, (h + 1) * RET_D)
        o = oret_ref[:, sl]
        mu = jnp.mean(o, axis=-1, keepdims=True)
        var = jnp.mean(jnp.square(o - mu), axis=-1, keepdims=True)
        r = (o - mu) * lax.rsqrt(var + LN_EPS) * gn_ref[0, :, sl]
        g = g_ref[:, sl]
        parts.append(g * jax.nn.sigmoid(g) * r)
    cat = jnp.concatenate(parts + [oatt_ref[...]], axis=1).astype(BF16)
    mix = jnp.dot(cat, wob_ref[...], preferred_element_type=F32)
    x1 = _layer_norm(alpha * x_ref[...] + mix, lg_ref[0], lb_ref[0])
    x1_ref[...] = x1

    logits = lax.dot_general(rwt_ref[...], x1, (((1,), (1,)), ((), ())),
                             precision=lax.Precision.HIGHEST, preferred_element_type=F32)
    aff = jax.nn.sigmoid(logits)
    biased = aff + rb_ref[...]
    gates = _route_rows([aff[e:e + 1] for e in range(N_EXPERTS)],
                        [biased[e:e + 1] for e in range(N_EXPERTS)])
    tm = x1.shape[0]
    gt = jnp.concatenate(gates + [jnp.zeros((LANES - N_EXPERTS, tm), F32)], axis=0)
    gates_ref[...] = gt.T


def _merge(o_ret, pm, o_att, x2d, gn_w, w_out, ln_g, ln_b, rwt, rb_col, layer, tm, alpha):
    n = x2d.shape[0]
    W = RET_HEADS * RET_D
    vec = lambda width: pl.BlockSpec((1, 1, width), lambda i: (layer, 0, 0))
    return pl.pallas_call(
        functools.partial(_merge_body, alpha=alpha),
        grid=(n // tm,),
        in_specs=[
            pl.BlockSpec((tm, W), lambda i: (i, 0)),
            pl.BlockSpec((tm, W), lambda i: (i, 3)),
            pl.BlockSpec((tm, W), lambda i: (i, 0)),
            pl.BlockSpec((tm, D_MODEL), lambda i: (i, 0)),
            vec(W),
            pl.BlockSpec((1, D_MODEL, D_MODEL), lambda i: (layer, 0, 0)),
            vec(D_MODEL), vec(D_MODEL),
            pl.BlockSpec((N_EXPERTS, D_MODEL), lambda i: (0, 0)),
            pl.BlockSpec((N_EXPERTS, 1), lambda i: (0, 0)),
        ],
        out_specs=[pl.BlockSpec((tm, D_MODEL), lambda i: (i, 0)),
                   pl.BlockSpec((tm, LANES), lambda i: (i, 0))],
        out_shape=[jax.ShapeDtypeStruct((n, D_MODEL), F32), jax.ShapeDtypeStruct((n, LANES), F32)],
        scratch_shapes=[pltpu.VMEM((D_MODEL, D_MODEL), BF16)],
        compiler_params=_cparams(("arbitrary",)),
        name="merge",
    )(o_ret, pm, o_att, x2d, gn_w, w_out, ln_g, ln_b, rwt, rb_col)


def _moe_body(x1_ref, gates_ref, wg_ref, wu_ref, wd_ref, lg_ref, lb_ref, out_ref, xb_ref, *, alpha):
    e = pl.program_id(1)

    @pl.when(e == 0)
    def _():
        xb_ref[...] = x1_ref[...].astype(BF16)
        out_ref[...] = jnp.zeros(out_ref.shape, F32)

    xb = xb_ref[...]
    a = jnp.dot(xb, wg_ref[0, 0].astype(BF16), preferred_element_type=F32)
    u = jnp.dot(xb, wu_ref[0, 0].astype(BF16), preferred_element_type=F32)
    hid = (a * jax.nn.sigmoid(a) * u).astype(BF16)
    y = jnp.dot(hid, wd_ref[0, 0].astype(BF16), preferred_element_type=F32)
    gates = gates_ref[...]
    lane = lax.broadcasted_iota(I32, gates.shape, 1)
    g = jnp.sum(jnp.where(lane == e, gates, 0.0), axis=1, keepdims=True)
    out_ref[...] += g * y

    @pl.when(e == N_EXPERTS - 1)
    def _():
        out_ref[...] = _layer_norm(alpha * x1_ref[...] + out_ref[...], lg_ref[0], lb_ref[0])


def _moe(x1, gates, w_gate, w_up, w_down, ln_g, ln_b, layer, tm, alpha):
    n = x1.shape[0]
    dff = w_gate.shape[-1]
    vec = pl.BlockSpec((1, 1, D_MODEL), lambda i, e: (layer, 0, 0))
    return pl.pallas_call(
        functools.partial(_moe_body, alpha=alpha),
        grid=(n // tm, N_EXPERTS),
        in_specs=[
            pl.BlockSpec((tm, D_MODEL), lambda i, e: (i, 0)),
            pl.BlockSpec((tm, LANES), lambda i, e: (i, 0)),
            pl.BlockSpec((1, 1, D_MODEL, dff), lambda i, e: (layer, e, 0, 0)),
            pl.BlockSpec((1, 1, D_MODEL, dff), lambda i, e: (layer, e, 0, 0)),
            pl.BlockSpec((1, 1, dff, D_MODEL), lambda i, e: (layer, e, 0, 0)),
            vec, vec,
        ],
        out_specs=pl.BlockSpec((tm, D_MODEL), lambda i, e: (i, 0)),
        out_shape=jax.ShapeDtypeStruct((n, D_MODEL), F32),
        scratch_shapes=[pltpu.VMEM((tm, D_MODEL), BF16)],
        compiler_params=_cparams(("arbitrary", "arbitrary")),
        name="moe",
    )(x1, gates, w_gate, w_up, w_down, ln_g, ln_b)


def _rope_tables(pos):
    half = RET_D // 2
    inv = ROPE_BASE ** (-jnp.arange(half, dtype=F32) / half)
    ang = pos.astype(F32)[:, None] * inv[None, :]
    cos, sin = jnp.cos(ang), jnp.sin(ang)
    return jnp.concatenate([cos, cos], axis=1), jnp.concatenate([-sin, sin], axis=1)


def kernel(x_prompt, x_sample, cache_k, cache_v, cache_kidx, state_ret, page_table, w_in, ret_gn_w, w_out,
           ln1_g, ln1_b, router_w, router_b, w_gate, w_up, w_down, ln2_g, ln2_b):
    depth = w_in.shape[0]
    B, T, D = x_prompt.shape
    DB, TS, _ = x_sample.shape
    n_pool = cache_k.shape[1]
    W = ATT_HEADS * ATT_HD
    past = page_table.shape[1] * PAGE
    alpha = (2 * depth) ** 0.25

    w_tail = jnp.pad(w_in[:, :, MAIN_W:], ((0, 0), (0, 0), (0, TAIL_W - (w_in.shape[2] - MAIN_W))))
    cache_k4 = cache_k.reshape(depth, n_pool, PAGE, W)
    cache_v4 = cache_v.reshape(depth, n_pool, PAGE, W)
    rwt = router_w.T
    rb_col = router_b.reshape(N_EXPERTS, 1)
    vec3 = lambda a: a.reshape(depth, 1, a.shape[-1])
    gn3, l1g, l1b, l2g, l2b = vec3(ret_gn_w), vec3(ln1_g), vec3(ln1_b), vec3(ln2_g), vec3(ln2_b)
    cos_p, sin_p = _rope_tables(jnp.arange(T))
    cos_s, sin_s = _rope_tables(past + jnp.arange(TS))
    zero_state = jnp.zeros((B, RET_HEADS, RET_D, RET_D), F32)
    C = min(LANES, T)

    def block(x2d, nb, nt, o_att_fn, cosf, sinf, state0, chunk, tm_proj, tm_merge, tm_moe, l):
        pm, pt = _project(x2d, w_in, w_tail, l, tm_proj)
        pm3 = pm.reshape(nb, nt, MAIN_W)
        pt3 = pt.reshape(nb, nt, TAIL_W)
        o_ret, st = _retention(pm3, cosf, sinf, state0, chunk)
        o_att = o_att_fn(pm3, pt3)
        x1, gates = _merge(o_ret.reshape(nb * nt, -1), pm, o_att.reshape(nb * nt, -1), x2d, gn3, w_out,
                           l1g, l1b, rwt, rb_col, l, tm_merge, alpha)
        x2 = _moe(x1, gates, w_gate, w_up, w_down, l2g, l2b, l, tm_moe, alpha)
        k_a = pm3[:, :, 5 * W:6 * W].reshape(nb, nt, ATT_HEADS, ATT_HD)
        v_a = pm3[:, :, 6 * W:7 * W].reshape(nb, nt, ATT_HEADS, ATT_HD)
        k_i = pt3[:, :, :IDX_DIM]
        return x2, k_a, v_a, k_i, st

    xp = x_prompt.reshape(B * T, D)
    xs = x_sample.reshape(DB * TS, D)
    outs_p, outs_s = [], []
    for l in range(depth):
        xp, *rest = block(xp, B, T, _dsa_prompt, cos_p, sin_p, zero_state, C,
                          min(1024, B * T), min(512, B * T), min(1024, B * T), l)
        outs_p.append(rest)
        sample_att = functools.partial(_dsa_sample, cache_k4=cache_k4, cache_v4=cache_v4, cache_ki=cache_kidx,
                                       page_table=page_table, layer=l)
        xs, *rest = block(xs, DB, TS, sample_att, cos_s, sin_s, state_ret[l], TS,
                          DB * TS, DB * TS, DB * TS, l)
        outs_s.append(rest)
    stack = lambda outs, i: jnp.stack([o[i] for o in outs])
    return (xp.reshape(B, T, D), xs.reshape(DB, TS, D),
            stack(outs_p, 0), stack(outs_p, 1), stack(outs_p, 2), stack(outs_p, 3),
            stack(outs_s, 0), stack(outs_s, 1), stack(outs_s, 2), stack(outs_s, 3))
```

```python
import functools

import numpy as np
import jax
import jax.numpy as jnp
from jax import lax
from jax.experimental import pallas as pl
from jax.experimental.pallas import tpu as pltpu

F32 = jnp.float32
BF16 = jnp.bfloat16
I32 = jnp.int32

RET_HEADS = 4
RET_D = 128
ATT_HEADS = 8
ATT_HD = 64
IDX_HEADS = 8
IDX_DIM = 64
TOPK_MAX = 256
PAGE = 128
N_EXPERTS = 16
N_GROUPS = 4
GROUP_SIZE = N_EXPERTS // N_GROUPS
ROPE_BASE = 10000.0
LN_EPS = 1e-5
D_MODEL = 1024
MAIN_W = 4096
TAIL_W = 128
LANES = 128
VMEM_LIMIT = 56 * 1024 * 1024

INT_MIN = -(2 ** 31)
NEG_INF = float("-inf")


def _log_gammas():
    h = np.arange(RET_HEADS, dtype=np.float32)
    return [float(v) for v in np.log1p(-np.exp2(-5.0 - h)).astype(np.float32)]


def _cparams(sem):
    return pltpu.CompilerParams(dimension_semantics=sem, vmem_limit_bytes=VMEM_LIMIT)


def _proj_body(x_ref, wm_ref, wt_ref, pm_ref, pt_ref, xb_ref):
    @pl.when(pl.program_id(1) == 0)
    def _():
        xb = x_ref[...].astype(BF16)
        xb_ref[...] = xb
        pt_ref[...] = jnp.dot(xb, wt_ref[0].astype(BF16), preferred_element_type=F32)

    pm_ref[...] = jnp.dot(xb_ref[...], wm_ref[0].astype(BF16), preferred_element_type=F32)


def _project(x2d, w_in, w_tail, layer, tm):
    n = x2d.shape[0]
    tn = 512
    return pl.pallas_call(
        _proj_body,
        grid=(n // tm, MAIN_W // tn),
        in_specs=[
            pl.BlockSpec((tm, D_MODEL), lambda i, j: (i, 0)),
            pl.BlockSpec((1, D_MODEL, tn), lambda i, j: (layer, 0, j)),
            pl.BlockSpec((1, D_MODEL, TAIL_W), lambda i, j: (layer, 0, 0)),
        ],
        out_specs=[
            pl.BlockSpec((tm, tn), lambda i, j: (i, j)),
            pl.BlockSpec((tm, TAIL_W), lambda i, j: (i, 0)),
        ],
        out_shape=[jax.ShapeDtypeStruct((n, MAIN_W), F32), jax.ShapeDtypeStruct((n, TAIL_W), F32)],
        scratch_shapes=[pltpu.VMEM((tm, D_MODEL), BF16)],
        compiler_params=_cparams(("arbitrary", "arbitrary")),
        name="proj",
    )(x2d, w_in, w_tail)


def _ret_body(q_ref, k_ref, v_ref, cos_ref, sin_ref, s0_ref, o_ref, sout_ref, st_ref, *, C, NC):
    c = pl.program_id(1)
    CP = max(C, LANES)

    @pl.when(c == 0)
    def _():
        st_ref[...] = s0_ref[0]

    def pad(a):
        if CP == C:
            return a
        return jnp.concatenate([a, jnp.zeros((CP - C, a.shape[1]), a.dtype)], axis=0)

    cosf = pad(cos_ref[...])
    sinf = pad(sin_ref[...])
    ii = lax.broadcasted_iota(I32, (CP, CP), 0)
    jj = lax.broadcasted_iota(I32, (CP, CP), 1)
    diff = (ii - jj).astype(F32)
    pos = lax.broadcasted_iota(I32, (CP, 1), 0).astype(F32)
    for h, lg in enumerate(_log_gammas()):
        sl = slice(h * RET_D, (h + 1) * RET_D)
        q = pad(q_ref[0, :, sl])
        k = pad(k_ref[0, :, sl])
        vb = pad(v_ref[0, :, sl]).astype(BF16)
        q = q * cosf + pltpu.roll(q, RET_D // 2, 1) * sinf
        k = (k * cosf + pltpu.roll(k, RET_D // 2, 1) * sinf) * (RET_D ** -0.5)
        decay = jnp.where(diff >= 0, jnp.exp(lg * jnp.maximum(diff, 0.0)), 0.0)
        qb = q.astype(BF16)
        scores = lax.dot_general(qb, k.astype(BF16), (((1,), (1,)), ((), ())),
                                 preferred_element_type=F32) * decay
        inner = jnp.dot(scores.astype(BF16), vb, preferred_element_type=F32)
        st = st_ref[h]
        cross = jnp.dot(qb, st.astype(BF16), preferred_element_type=F32) * jnp.exp(lg * (pos + 1.0))
        o_ref[0, :, sl] = (inner + cross)[:C]
        kd = k * jnp.exp(lg * (C - 1.0 - pos))
        st_ref[h] = st * float(np.exp(np.float32(lg) * np.float32(C))) + jnp.dot(
            kd.T.astype(BF16), vb, preferred_element_type=F32)

    @pl.when(c == NC - 1)
    def _():
        sout_ref[0] = st_ref[...]


def _retention(pm3, cosf, sinf, state0, C):
    B, T, _ = pm3.shape
    NC = T // C
    W = RET_HEADS * RET_D
    return pl.pallas_call(
        functools.partial(_ret_body, C=C, NC=NC),
        grid=(B, NC),
        in_specs=[
            pl.BlockSpec((1, C, W), lambda b, c: (b, c, 0)),
            pl.BlockSpec((1, C, W), lambda b, c: (b, c, 1)),
            pl.BlockSpec((1, C, W), lambda b, c: (b, c, 2)),
            pl.BlockSpec((C, RET_D), lambda b, c: (c, 0)),
            pl.BlockSpec((C, RET_D), lambda b, c: (c, 0)),
            pl.BlockSpec((1, RET_HEADS, RET_D, RET_D), lambda b, c: (b, 0, 0, 0)),
        ],
        out_specs=[
            pl.BlockSpec((1, C, W), lambda b, c: (b, c, 0)),
            pl.BlockSpec((1, RET_HEADS, RET_D, RET_D), lambda b, c: (b, 0, 0, 0)),
        ],
        out_shape=[jax.ShapeDtypeStruct((B, T, W), F32),
                   jax.ShapeDtypeStruct((B, RET_HEADS, RET_D, RET_D), F32)],
        scratch_shapes=[pltpu.VMEM((RET_HEADS, RET_D, RET_D), F32)],
        compiler_params=_cparams(("arbitrary", "arbitrary")),
        name="retention",
    )(pm3, pm3, pm3, cosf, sinf, state0)


def _sortable(x):
    bits = pltpu.bitcast(x + 0.0, I32)
    return bits ^ ((bits >> 31) & 0x7FFFFFFF)


def _kth_largest(count_ge, shape, kf):
    c0 = count_ge(jnp.zeros(shape, I32))
    tau = jnp.where(c0 >= kf, 0, INT_MIN).astype(I32)

    def body(it, tau):
        cand = tau | (jnp.int32(1) << (30 - it))
        return jnp.where(count_ge(cand) >= kf, cand, tau)

    return lax.fori_loop(0, 31, body, tau)


def _tie_cutoff(count_lt, shape, need, nbits):
    def body(it, m):
        cand = m + (jnp.int32(1) << (nbits - 1 - it))
        return jnp.where(count_lt(cand) < need, cand, m)

    return lax.fori_loop(0, nbits, body, jnp.zeros(shape, I32))


def _col_reduce(x, op):
    S, Q = x.shape
    if S > LANES:
        x = op(x.reshape(S // LANES, LANES, Q), axis=0)
    return op(x, axis=0, keepdims=True)


def _dsa_prompt_block(i, S, qa_ref, qi_ref, ptq_ref, o_ref, kb_ref, vt_ref, kib_ref, keys_ref, bias_ref,
                      *, QB, ksel):
    RC = 256 if S % 256 == 0 else LANES
    qiT = qi_ref[0].T
    ptT = ptq_ref[0].T
    zeros64 = jnp.zeros((IDX_DIM, QB), BF16)
    rhs_idx = jnp.concatenate(
        [jnp.concatenate([qiT[h * IDX_DIM:(h + 1) * IDX_DIM].astype(BF16), zeros64], axis=0)
         for h in range(IDX_HEADS)], axis=1)
    w_rows = [ptT[IDX_DIM + h:IDX_DIM + h + 1, :] * (IDX_HEADS ** -0.5) * (IDX_DIM ** -0.5)
              for h in range(IDX_HEADS)]
    t_pos = i * QB + lax.broadcasted_iota(I32, (RC, QB), 1)
    for r in range(S // RC):
        d = jnp.dot(kib_ref[r * RC:(r + 1) * RC, :], rhs_idx, preferred_element_type=F32)
        acc = jnp.maximum(d[:, 0:QB], 0.0) * w_rows[0]
        for h in range(1, IDX_HEADS):
            acc = acc + jnp.maximum(d[:, h * QB:(h + 1) * QB], 0.0) * w_rows[h]
        s_pos = r * RC + lax.broadcasted_iota(I32, (RC, QB), 0)
        keys_ref[r * RC:(r + 1) * RC, :] = _sortable(jnp.where(s_pos <= t_pos, acc, NEG_INF))

    kf = float(ksel)
    NT = S // LANES

    def count(pred):
        acc = jnp.zeros((LANES, QB), F32)
        for r in range(NT):
            acc = jnp.where(pred(keys_ref[r * LANES:(r + 1) * LANES, :], r * LANES), acc + 1.0, acc)
        return jnp.sum(acc, axis=0, keepdims=True)

    tau = _kth_largest(lambda cand: count(lambda k, _: k >= cand), (1, QB), kf)
    row = lax.broadcasted_iota(I32, (LANES, QB), 0)
    t_row = i * QB + lax.broadcasted_iota(I32, (LANES, QB), 1)
    need = kf - count(lambda k, _: k > tau)
    n_eq = count(lambda k, r0: (k == tau) & (r0 + row <= t_row))
    any_split = jnp.max(jnp.where(n_eq > need, 1.0, 0.0)) > 0.0

    def slow():
        return _tie_cutoff(lambda m: count(lambda k, r0: (k == tau) & (r0 + row <= t_row) & (r0 + row < m)),
                           (1, QB), need, int(S).bit_length())

    mcut = lax.cond(any_split, slow, lambda: jnp.full((1, QB), S, I32))
    for r in range(NT):
        k = keys_ref[r * LANES:(r + 1) * LANES, :]
        s_pos = r * LANES + row
        sel = (s_pos <= t_row) & ((k > tau) | ((k == tau) & (s_pos <= mcut)))
        bias_ref[r * LANES:(r + 1) * LANES, :] = jnp.where(sel, 0.0, NEG_INF)

    qaT = qa_ref[0].T * (ATT_HD ** -0.5)
    zq = jnp.zeros((ATT_HD, QB), BF16)
    outs = []
    for j in range(ATT_HEADS // 2):
        r0 = 2 * j * ATT_HD
        top = jnp.concatenate([qaT[r0:r0 + ATT_HD].astype(BF16), zq], axis=1)
        bot = jnp.concatenate([zq, qaT[r0 + ATT_HD:r0 + 2 * ATT_HD].astype(BF16)], axis=1)
        rhs = jnp.concatenate([top, bot], axis=0)
        sT = jnp.dot(kb_ref[0:S, r0:r0 + 2 * ATT_HD], rhs, preferred_element_type=F32)
        for u in range(2):
            h = 2 * j + u
            s = sT[:, u * QB:(u + 1) * QB] + bias_ref[0:S, :]
            m = _col_reduce(s, jnp.max)
            p = jnp.exp(s - m)
            l = _col_reduce(p, jnp.sum)
            oT = jnp.dot(vt_ref[h * ATT_HD:(h + 1) * ATT_HD, 0:S], p.astype(BF16),
                         preferred_element_type=F32)
            outs.append(oT / l)
    o_ref[0] = jnp.concatenate(outs, axis=0).T


def _dsa_prompt_body(qa_ref, k_ref, v_ref, qi_ref, pt_ref, ptq_ref, o_ref,
                     kb_ref, vt_ref, kib_ref, keys_ref, bias_ref, *, T, QB, ksel, NV):
    i = pl.program_id(1)

    @pl.when(i == 0)
    def _():
        kb_ref[...] = k_ref[0].astype(BF16)
        vt_ref[...] = v_ref[0].T.astype(BF16)
        kib_ref[...] = pt_ref[0].astype(BF16)

    per = (T // QB) // NV
    for g in range(NV):
        @pl.when(i // per == g)
        def _(g=g):
            _dsa_prompt_block(i, (g + 1) * per * QB, qa_ref, qi_ref, ptq_ref, o_ref, kb_ref, vt_ref, kib_ref,
                              keys_ref, bias_ref, QB=QB, ksel=ksel)


def _dsa_prompt(pm3, pt3):
    B, T, _ = pm3.shape
    QB = min(LANES, T)
    ksel = min(TOPK_MAX, T // 4)
    W = ATT_HEADS * ATT_HD
    NV = min(4, T // QB)
    return pl.pallas_call(
        functools.partial(_dsa_prompt_body, T=T, QB=QB, ksel=ksel, NV=NV),
        grid=(B, T // QB),
        in_specs=[
            pl.BlockSpec((1, QB, W), lambda b, i: (b, i, 4)),
            pl.BlockSpec((1, T, W), lambda b, i: (b, 0, 5)),
            pl.BlockSpec((1, T, W), lambda b, i: (b, 0, 6)),
            pl.BlockSpec((1, QB, W), lambda b, i: (b, i, 7)),
            pl.BlockSpec((1, T, TAIL_W), lambda b, i: (b, 0, 0)),
            pl.BlockSpec((1, QB, TAIL_W), lambda b, i: (b, i, 0)),
        ],
        out_specs=pl.BlockSpec((1, QB, W), lambda b, i: (b, i, 0)),
        out_shape=jax.ShapeDtypeStruct((B, T, W), F32),
        scratch_shapes=[pltpu.VMEM((T, W), BF16), pltpu.VMEM((W, T), BF16),
                        pltpu.VMEM((T, TAIL_W), BF16), pltpu.VMEM((T, QB), I32),
                        pltpu.VMEM((T, QB), F32)],
        compiler_params=_cparams(("arbitrary", "arbitrary")),
        name="dsa_prompt",
    )(pm3, pm3, pm3, pm3, pt3, pt3)


PAGES_PER_STEP = 8


def _idx_sample_body(ptab_ref, *refs, P):
    del ptab_ref
    page_refs = refs[:P]
    qi_ref, w_ref, kin_ref, sc_ref, scn_ref = refs[P:]
    q = qi_ref[0].astype(BF16)
    w = w_ref[0] * (IDX_HEADS ** -0.5) * (IDX_DIM ** -0.5)
    nq = q.shape[0] // IDX_HEADS

    def scores(keys_t):
        d = jnp.dot(q, keys_t.astype(BF16), preferred_element_type=F32)
        r = jnp.maximum(d, 0.0) * w
        acc = r[0:nq]
        for h in range(1, IDX_HEADS):
            acc = acc + r[h * nq:(h + 1) * nq]
        return acc

    sc_ref[0, 0] = scores(jnp.concatenate([r[0, 0] for r in page_refs], axis=1))

    @pl.when(pl.program_id(1) == 0)
    def _():
        scn_ref[0] = scores(kin_ref[0])


def _idx_sample(page_table, cache_ki_t, layer, qi_flat, w_col, ki_new_t):
    DB, n_pages = page_table.shape
    P = PAGES_PER_STEP
    NCH = n_pages // P
    nq = qi_flat.shape[1] // IDX_HEADS

    def page_spec(p):
        return pl.BlockSpec((1, 1, IDX_DIM, PAGE), lambda b, c, pt: (layer, pt[b, c * P + p], 0, 0))

    grid_spec = pltpu.PrefetchScalarGridSpec(
        num_scalar_prefetch=1,
        grid=(DB, NCH),
        in_specs=[page_spec(p) for p in range(P)] + [
            pl.BlockSpec((1, IDX_HEADS * nq, IDX_DIM), lambda b, c, pt: (b, 0, 0)),
            pl.BlockSpec((1, IDX_HEADS * nq, 1), lambda b, c, pt: (b, 0, 0)),
            pl.BlockSpec((1, IDX_DIM, PAGE), lambda b, c, pt: (b, 0, 0)),
        ],
        out_specs=[
            pl.BlockSpec((1, 1, nq, P * PAGE), lambda b, c, pt: (b, c, 0, 0)),
            pl.BlockSpec((1, nq, PAGE), lambda b, c, pt: (b, 0, 0)),
        ],
    )
    return pl.pallas_call(
        functools.partial(_idx_sample_body, P=P),
        grid_spec=grid_spec,
        out_shape=[jax.ShapeDtypeStruct((DB, NCH, nq, P * PAGE), F32),
                   jax.ShapeDtypeStruct((DB, nq, PAGE), F32)],
        compiler_params=_cparams(("arbitrary", "arbitrary")),
        name="idx_sample",
    )(page_table, *([cache_ki_t] * P), qi_flat, w_col, ki_new_t)


def _select_sample_body(sc_ref, scn_ref, bm_ref, bn_ref, keys_ref, *, NCH, nq, past, ksel):
    S = past + PAGE
    main = jnp.concatenate([sc_ref[0, c] for c in range(NCH)], axis=1)
    t_new = lax.broadcasted_iota(I32, (nq, PAGE), 0)
    s_new = lax.broadcasted_iota(I32, (nq, PAGE), 1)
    new = jnp.where(s_new <= t_new, scn_ref[0], NEG_INF)
    keys_ref[...] = _sortable(jnp.concatenate([main, new], axis=1))
    pos = lax.broadcasted_iota(I32, (nq, S), 1)
    valid = (pos < past) | ((pos - past) <= lax.broadcasted_iota(I32, (nq, S), 0))
    kf = float(ksel)

    def count_ge(cand):
        return jnp.sum(jnp.where(keys_ref[...] >= cand, 1.0, 0.0), axis=1, keepdims=True)

    tau = _kth_largest(count_ge, (nq, 1), kf)
    keys = keys_ref[...]
    gt = keys > tau
    eq = (keys == tau) & valid
    need = kf - jnp.sum(jnp.where(gt, 1.0, 0.0), axis=1, keepdims=True)
    n_eq = jnp.sum(jnp.where(eq, 1.0, 0.0), axis=1, keepdims=True)
    any_split = jnp.max(jnp.where(n_eq > need, 1.0, 0.0)) > 0.0

    def slow():
        def count_lt(m):
            hit = (keys_ref[...] == tau) & valid & (pos < m)
            return jnp.sum(jnp.where(hit, 1.0, 0.0), axis=1, keepdims=True)
        return _tie_cutoff(count_lt, (nq, 1), need, int(S).bit_length())

    mcut = lax.cond(any_split, slow, lambda: jnp.full((nq, 1), S, I32))
    sel = valid & (gt | (eq & (pos <= mcut)))
    bias = jnp.where(sel, 0.0, NEG_INF)
    bm_ref[0] = bias[:, :past]
    bn_ref[0] = bias[:, past:]


def _select_sample(sc, scn, ksel):
    DB, NCH, nq, CW = sc.shape
    past = NCH * CW
    return pl.pallas_call(
        functools.partial(_select_sample_body, NCH=NCH, nq=nq, past=past, ksel=ksel),
        grid=(DB,),
        in_specs=[pl.BlockSpec((1, NCH, nq, CW), lambda b: (b, 0, 0, 0)),
                  pl.BlockSpec((1, nq, PAGE), lambda b: (b, 0, 0))],
        out_specs=[pl.BlockSpec((1, nq, past), lambda b: (b, 0, 0)),
                   pl.BlockSpec((1, nq, PAGE), lambda b: (b, 0, 0))],
        out_shape=[jax.ShapeDtypeStruct((DB, nq, past), F32),
                   jax.ShapeDtypeStruct((DB, nq, PAGE), F32)],
        scratch_shapes=[pltpu.VMEM((nq, past + PAGE), I32)],
        compiler_params=_cparams(("arbitrary",)),
        name="select_sample",
    )(sc, scn)


def _att_sample_body(ptab_ref, *refs, P, NCH, nq):
    del ptab_ref
    k_refs = refs[:P]
    v_refs = refs[P:2 * P]
    q_ref, bm_ref, bn_ref, kn_ref, vn_ref, o_ref, m_ref, l_ref, acc_ref = refs[2 * P:]
    c = pl.program_id(1)

    @pl.when(c == 0)
    def _():
        m_ref[...] = jnp.full(m_ref.shape, NEG_INF, F32)
        l_ref[...] = jnp.zeros(l_ref.shape, F32)
        acc_ref[...] = jnp.zeros(acc_ref.shape, F32)

    q = (q_ref[0] * (ATT_HD ** -0.5)).astype(BF16)

    def update(k_t, v_t, bias):
        s = jnp.dot(q, k_t.astype(BF16), preferred_element_type=F32)
        s = s + jnp.concatenate([bias] * ATT_HEADS, axis=0)
        m_old = m_ref[...]
        m_new = jnp.maximum(m_old, jnp.max(s, axis=1, keepdims=True))
        m_safe = jnp.where(m_new == NEG_INF, 0.0, m_new)
        alpha = jnp.exp(m_old - m_safe)
        p = jnp.exp(s - m_safe)
        l_ref[...] = l_ref[...] * alpha + jnp.sum(p, axis=1, keepdims=True)
        acc_ref[...] = acc_ref[...] * alpha + lax.dot_general(
            p.astype(BF16), v_t.astype(BF16), (((1,), (1,)), ((), ())), preferred_element_type=F32)
        m_ref[...] = m_new

    update(jnp.concatenate([r[0, 0] for r in k_refs], axis=1),
           jnp.concatenate([r[0, 0] for r in v_refs], axis=1), bm_ref[0])

    @pl.when(c == NCH - 1)
    def _():
        update(kn_ref[0], vn_ref[0], bn_ref[0])
        o = acc_ref[...] / l_ref[...]
        head_of_lane = lax.broadcasted_iota(I32, (nq, ATT_HEADS * ATT_HD), 1) // ATT_HD
        out = jnp.zeros((nq, ATT_HEADS * ATT_HD), F32)
        for h in range(ATT_HEADS):
            out = out + jnp.where(head_of_lane == h, o[h * nq:(h + 1) * nq], 0.0)
        o_ref[0] = out


def _att_sample(page_table, cache_k_t, cache_v_t, layer, q_bd, bias_main, bias_new, k_new_t, v_new_t):
    DB, n_pages = page_table.shape
    P = PAGES_PER_STEP
    NCH = n_pages // P
    nq = bias_main.shape[1]
    W = ATT_HEADS * ATT_HD

    def page_spec(p):
        return pl.BlockSpec((1, 1, W, PAGE), lambda b, c, pt: (layer, pt[b, c * P + p], 0, 0))

    grid_spec = pltpu.PrefetchScalarGridSpec(
        num_scalar_prefetch=1,
        grid=(DB, NCH),
        in_specs=[page_spec(p) for p in range(P)] + [page_spec(p) for p in range(P)] + [
            pl.BlockSpec((1, ATT_HEADS * nq, W), lambda b, c, pt: (b, 0, 0)),
            pl.BlockSpec((1, nq, P * PAGE), lambda b, c, pt: (b, 0, c)),
            pl.BlockSpec((1, nq, PAGE), lambda b, c, pt: (b, 0, 0)),
            pl.BlockSpec((1, W, PAGE), lambda b, c, pt: (b, 0, 0)),
            pl.BlockSpec((1, W, PAGE), lambda b, c, pt: (b, 0, 0)),
        ],
        out_specs=pl.BlockSpec((1, nq, W), lambda b, c, pt: (b, 0, 0)),
        scratch_shapes=[pltpu.VMEM((ATT_HEADS * nq, 1), F32), pltpu.VMEM((ATT_HEADS * nq, 1), F32),
                        pltpu.VMEM((ATT_HEADS * nq, W), F32)],
    )
    return pl.pallas_call(
        functools.partial(_att_sample_body, P=P, NCH=NCH, nq=nq),
        grid_spec=grid_spec,
        out_shape=jax.ShapeDtypeStruct((DB, nq, W), F32),
        compiler_params=_cparams(("arbitrary", "arbitrary")),
        name="att_sample",
    )(page_table, *([cache_k_t] * P), *([cache_v_t] * P), q_bd, bias_main, bias_new, k_new_t, v_new_t)


def _dsa_sample(pm3, pt3, cache_k_t, cache_v_t, cache_ki_t, page_table, layer):
    DB, nq, _ = pm3.shape
    W = ATT_HEADS * ATT_HD
    past = page_table.shape[1] * PAGE
    ksel = min(TOPK_MAX, (past + nq) // 4)
    q_a = pm3[:, :, 4 * W:5 * W].reshape(DB, nq, ATT_HEADS, ATT_HD)
    q_i = pm3[:, :, 7 * W:8 * W].reshape(DB, nq, IDX_HEADS, IDX_DIM)
    qi_flat = q_i.transpose(0, 2, 1, 3).reshape(DB, IDX_HEADS * nq, IDX_DIM)
    w_col = pt3[:, :, IDX_DIM:IDX_DIM + IDX_HEADS].transpose(0, 2, 1).reshape(DB, IDX_HEADS * nq, 1)
    q_bd = jnp.einsum("bthd,hg->bhtgd", q_a, jnp.eye(ATT_HEADS, dtype=F32)).reshape(DB, ATT_HEADS * nq, W)

    def new_t(a):
        return jnp.pad(a.transpose(0, 2, 1), ((0, 0), (0, 0), (0, PAGE - nq)))

    sc, scn = _idx_sample(page_table, cache_ki_t, layer, qi_flat, w_col, new_t(pt3[:, :, :IDX_DIM]))
    bias_main, bias_new = _select_sample(sc, scn, ksel)
    return _att_sample(page_table, cache_k_t, cache_v_t, layer, q_bd, bias_main, bias_new,
                       new_t(pm3[:, :, 5 * W:6 * W]), new_t(pm3[:, :, 6 * W:7 * W]))


def _layer_norm(y, g, b):
    mu = jnp.mean(y, axis=-1, keepdims=True)
    var = jnp.mean(jnp.square(y - mu), axis=-1, keepdims=True)
    return (y - mu) * lax.rsqrt(var + LN_EPS) * g + b


def _top2_of4(b0, b1, b2, b3):
    hi01, lo01 = jnp.maximum(b0, b1), jnp.minimum(b0, b1)
    hi23, lo23 = jnp.maximum(b2, b3), jnp.minimum(b2, b3)
    return jnp.maximum(hi01, hi23), jnp.maximum(jnp.minimum(hi01, hi23), jnp.maximum(lo01, lo23))


def _route_rows(aff, biased):
    gscore = []
    for g in range(N_GROUPS):
        m1, m2 = _top2_of4(*biased[g * GROUP_SIZE:(g + 1) * GROUP_SIZE])
        gscore.append(m1 + m2)
    gmax = functools.reduce(jnp.maximum, gscore)
    taken = jnp.zeros_like(gmax, dtype=jnp.bool_)
    gsel = []
    for g in range(N_GROUPS):
        hit = (gscore[g] == gmax) & jnp.logical_not(taken)
        gsel.append(hit)
        taken = taken | hit
    cand, caff = [], []
    for k in range(GROUP_SIZE):
        c = jnp.full_like(gmax, NEG_INF)
        a = jnp.zeros_like(gmax)
        for g in range(N_GROUPS):
            c = jnp.where(gsel[g], biased[g * GROUP_SIZE + k], c)
            a = jnp.where(gsel[g], aff[g * GROUP_SIZE + k], a)
        cand.append(c)
        caff.append(a)
    best = functools.reduce(jnp.maximum, cand)
    taken = jnp.zeros_like(taken)
    first = []
    for k in range(GROUP_SIZE):
        hit = (cand[k] == best) & jnp.logical_not(taken)
        first.append(hit)
        taken = taken | hit
    rest = [jnp.where(first[k], NEG_INF, cand[k]) for k in range(GROUP_SIZE)]
    best2 = functools.reduce(jnp.maximum, rest)
    taken = jnp.zeros_like(taken)
    second = []
    for k in range(GROUP_SIZE):
        hit = (rest[k] == best2) & jnp.logical_not(first[k]) & jnp.logical_not(taken)
        second.append(hit)
        taken = taken | hit
    a1 = functools.reduce(jnp.add, [jnp.where(first[k], caff[k], 0.0) for k in range(GROUP_SIZE)])
    a2 = functools.reduce(jnp.add, [jnp.where(second[k], caff[k], 0.0) for k in range(GROUP_SIZE)])
    tot = a1 + a2
    w1, w2 = a1 / tot, a2 / tot
    gates = []
    for g in range(N_GROUPS):
        for k in range(GROUP_SIZE):
            gates.append(jnp.where(gsel[g] & first[k], w1, jnp.where(gsel[g] & second[k], w2, 0.0)))
    return gates


def _merge_body(oret_ref, g_ref, oatt_ref, x_ref, gn_ref, wout_ref, lg_ref, lb_ref, rwt_ref, rb_ref,
                x1_ref, gates_ref, wob_ref, *, alpha):
    @pl.when(pl.program_id(0) == 0)
    def _():
        wob_ref[...] = wout_ref[0].astype(BF16)

    parts = []
    for h in range(RET_HEADS):
        sl = slice(h * RET_D, (h + 1) * RET_D)
        o = oret_ref[:, sl]
        mu = jnp.mean(o, axis=-1, keepdims=True)
        var = jnp.mean(jnp.square(o - mu), axis=-1, keepdims=True)
        r = (o - mu) * lax.rsqrt(var + LN_EPS) * gn_ref[0, :, sl]
        g = g_ref[:, sl]
        parts.append(g * jax.nn.sigmoid(g) * r)
    cat = jnp.concatenate(parts + [oatt_ref[...]], axis=1).astype(BF16)
    mix = jnp.dot(cat, wob_ref[...], preferred_element_type=F32)
    x1 = _layer_norm(alpha * x_ref[...] + mix, lg_ref[0], lb_ref[0])
    x1_ref[...] = x1

    logits = lax.dot_general(rwt_ref[...], x1, (((1,), (1,)), ((), ())),
                             precision=lax.Precision.HIGHEST, preferred_element_type=F32)
    aff = jax.nn.sigmoid(logits)
    biased = aff + rb_ref[...]
    gates = _route_rows([aff[e:e + 1] for e in range(N_EXPERTS)],
                        [biased[e:e + 1] for e in range(N_EXPERTS)])
    tm = x1.shape[0]
    gt = jnp.concatenate(gates + [jnp.zeros((LANES - N_EXPERTS, tm), F32)], axis=0)
    gates_ref[...] = gt.T


def _merge(o_ret, pm, o_att, x2d, gn_w, w_out, ln_g, ln_b, rwt, rb_col, layer, tm, alpha):
    n = x2d.shape[0]
    W = RET_HEADS * RET_D
    vec = lambda width: pl.BlockSpec((1, 1, width), lambda i: (layer, 0, 0))
    return pl.pallas_call(
        functools.partial(_merge_body, alpha=alpha),
        grid=(n // tm,),
        in_specs=[
            pl.BlockSpec((tm, W), lambda i: (i, 0)),
            pl.BlockSpec((tm, W), lambda i: (i, 3)),
            pl.BlockSpec((tm, W), lambda i: (i, 0)),
            pl.BlockSpec((tm, D_MODEL), lambda i: (i, 0)),
            vec(W),
            pl.BlockSpec((1, D_MODEL, D_MODEL), lambda i: (layer, 0, 0)),
            vec(D_MODEL), vec(D_MODEL),
            pl.BlockSpec((N_EXPERTS, D_MODEL), lambda i: (0, 0)),
            pl.BlockSpec((N_EXPERTS, 1), lambda i: (0, 0)),
        ],
        out_specs=[pl.BlockSpec((tm, D_MODEL), lambda i: (i, 0)),
                   pl.BlockSpec((tm, LANES), lambda i: (i, 0))],
        out_shape=[jax.ShapeDtypeStruct((n, D_MODEL), F32), jax.ShapeDtypeStruct((n, LANES), F32)],
        scratch_shapes=[pltpu.VMEM((D_MODEL, D_MODEL), BF16)],
        compiler_params=_cparams(("arbitrary",)),
        name="merge",
    )(o_ret, pm, o_att, x2d, gn_w, w_out, ln_g, ln_b, rwt, rb_col)


def _moe_body(x1_ref, gates_ref, wg_ref, wu_ref, wd_ref, lg_ref, lb_ref, out_ref, xb_ref, *, alpha):
    e = pl.program_id(1)

    @pl.when(e == 0)
    def _():
        xb_ref[...] = x1_ref[...].astype(BF16)
        out_ref[...] = jnp.zeros(out_ref.shape, F32)

    xb = xb_ref[...]
    a = jnp.dot(xb, wg_ref[0, 0].astype(BF16), preferred_element_type=F32)
    u = jnp.dot(xb, wu_ref[0, 0].astype(BF16), preferred_element_type=F32)
    hid = (a * jax.nn.sigmoid(a) * u).astype(BF16)
    y = jnp.dot(hid, wd_ref[0, 0].astype(BF16), preferred_element_type=F32)
    gates = gates_ref[...]
    lane = lax.broadcasted_iota(I32, gates.shape, 1)
    g = jnp.sum(jnp.where(lane == e, gates, 0.0), axis=1, keepdims=True)
    out_ref[...] += g * y

    @pl.when(e == N_EXPERTS - 1)
    def _():
        out_ref[...] = _layer_norm(alpha * x1_ref[...] + out_ref[...], lg_ref[0], lb_ref[0])


def _moe(x1, gates, w_gate, w_up, w_down, ln_g, ln_b, layer, tm, alpha):
    n = x1.shape[0]
    dff = w_gate.shape[-1]
    vec = pl.BlockSpec((1, 1, D_MODEL), lambda i, e: (layer, 0, 0))
    return pl.pallas_call(
        functools.partial(_moe_body, alpha=alpha),
        grid=(n // tm, N_EXPERTS),
        in_specs=[
            pl.BlockSpec((tm, D_MODEL), lambda i, e: (i, 0)),
            pl.BlockSpec((tm, LANES), lambda i, e: (i, 0)),
            pl.BlockSpec((1, 1, D_MODEL, dff), lambda i, e: (layer, e, 0, 0)),
            pl.BlockSpec((1, 1, D_MODEL, dff), lambda i, e: (layer, e, 0, 0)),
            pl.BlockSpec((1, 1, dff, D_MODEL), lambda i, e: (layer, e, 0, 0)),
            vec, vec,
        ],
        out_specs=pl.BlockSpec((tm, D_MODEL), lambda i, e: (i, 0)),
        out_shape=jax.ShapeDtypeStruct((n, D_MODEL), F32),
        scratch_shapes=[pltpu.VMEM((tm, D_MODEL), BF16)],
        compiler_params=_cparams(("arbitrary", "arbitrary")),
        name="moe",
    )(x1, gates, w_gate, w_up, w_down, ln_g, ln_b)


def _rope_tables(pos):
    half = RET_D // 2
    inv = ROPE_BASE ** (-jnp.arange(half, dtype=F32) / half)
    ang = pos.astype(F32)[:, None] * inv[None, :]
    cos, sin = jnp.cos(ang), jnp.sin(ang)
    return jnp.concatenate([cos, cos], axis=1), jnp.concatenate([-sin, sin], axis=1)


def kernel(x_prompt, x_sample, cache_k, cache_v, cache_kidx, state_ret, page_table, w_in, ret_gn_w, w_out,
           ln1_g, ln1_b, router_w, router_b, w_gate, w_up, w_down, ln2_g, ln2_b):
    depth = w_in.shape[0]
    B, T, D = x_prompt.shape
    DB, TS, _ = x_sample.shape
    n_pool = cache_k.shape[1]
    W = ATT_HEADS * ATT_HD
    past = page_table.shape[1] * PAGE
    alpha = (2 * depth) ** 0.25

    w_tail = jnp.pad(w_in[:, :, MAIN_W:], ((0, 0), (0, 0), (0, TAIL_W - (w_in.shape[2] - MAIN_W))))
    cache_k_t = cache_k.transpose(0, 1, 3, 4, 2).reshape(depth, n_pool, W, PAGE)
    cache_v_t = cache_v.transpose(0, 1, 3, 4, 2).reshape(depth, n_pool, W, PAGE)
    cache_ki_t = cache_kidx.transpose(0, 1, 3, 2)
    rwt = router_w.T
    rb_col = router_b.reshape(N_EXPERTS, 1)
    vec3 = lambda a: a.reshape(depth, 1, a.shape[-1])
    gn3, l1g, l1b, l2g, l2b = vec3(ret_gn_w), vec3(ln1_g), vec3(ln1_b), vec3(ln2_g), vec3(ln2_b)
    cos_p, sin_p = _rope_tables(jnp.arange(T))
    cos_s, sin_s = _rope_tables(past + jnp.arange(TS))
    zero_state = jnp.zeros((B, RET_HEADS, RET_D, RET_D), F32)
    C = min(LANES, T)

    def block(x2d, nb, nt, o_att_fn, cosf, sinf, state0, chunk, tm_proj, tm_merge, tm_moe, l):
        pm, pt = _project(x2d, w_in, w_tail, l, tm_proj)
        pm3 = pm.reshape(nb, nt, MAIN_W)
        pt3 = pt.reshape(nb, nt, TAIL_W)
        o_ret, st = _retention(pm3, cosf, sinf, state0, chunk)
        o_att = o_att_fn(pm3, pt3)
        x1, gates = _merge(o_ret.reshape(nb * nt, -1), pm, o_att.reshape(nb * nt, -1), x2d, gn3, w_out,
                           l1g, l1b, rwt, rb_col, l, tm_merge, alpha)
        x2 = _moe(x1, gates, w_gate, w_up, w_down, l2g, l2b, l, tm_moe, alpha)
        k_a = pm3[:, :, 5 * W:6 * W].reshape(nb, nt, ATT_HEADS, ATT_HD)
        v_a = pm3[:, :, 6 * W:7 * W].reshape(nb, nt, ATT_HEADS, ATT_HD)
        k_i = pt3[:, :, :IDX_DIM]
        return x2, k_a, v_a, k_i, st

    xp = x_prompt.reshape(B * T, D)
    xs = x_sample.reshape(DB * TS, D)
    outs_p, outs_s = [], []
    for l in range(depth):
        xp, *rest = block(xp, B, T, _dsa_prompt, cos_p, sin_p, zero_state, C,
                          min(1024, B * T), min(512, B * T), min(1024, B * T), l)
        outs_p.append(rest)
        sample_att = functools.partial(_dsa_sample, cache_k_t=cache_k_t, cache_v_t=cache_v_t,
                                       cache_ki_t=cache_ki_t, page_table=page_table, layer=l)
        xs, *rest = block(xs, DB, TS, sample_att, cos_s, sin_s, state_ret[l], TS,
                          DB * TS, DB * TS, DB * TS, l)
        outs_s.append(rest)
    stack = lambda outs, i: jnp.stack([o[i] for o in outs])
    return (xp.reshape(B, T, D), xs.reshape(DB, TS, D),
            stack(outs_p, 0), stack(outs_p, 1), stack(outs_p, 2), stack(outs_p, 3),
            stack(outs_s, 0), stack(outs_s, 1), stack(outs_s, 2), stack(outs_s, 3))
```

```python
import functools

import numpy as np
import jax
import jax.numpy as jnp
from jax import lax
from jax.experimental import pallas as pl
from jax.experimental.pallas import tpu as pltpu

F32 = jnp.float32
BF16 = jnp.bfloat16
I32 = jnp.int32

RET_HEADS = 4
RET_D = 128
ATT_HEADS = 8
ATT_HD = 64
IDX_HEADS = 8
IDX_DIM = 64
TOPK_MAX = 256
PAGE = 128
N_EXPERTS = 16
N_GROUPS = 4
GROUP_SIZE = N_EXPERTS // N_GROUPS
ROPE_BASE = 10000.0
LN_EPS = 1e-5
D_MODEL = 1024
MAIN_W = 4096
TAIL_W = 128
LANES = 128
VMEM_LIMIT = 56 * 1024 * 1024

INT_MIN = -(2 ** 31)
NEG_INF = float("-inf")


def _log_gammas():
    h = np.arange(RET_HEADS, dtype=np.float32)
    return [float(v) for v in np.log1p(-np.exp2(-5.0 - h)).astype(np.float32)]


def _cparams(sem):
    return pltpu.CompilerParams(dimension_semantics=sem, vmem_limit_bytes=VMEM_LIMIT)


def _proj_body(x_ref, wm_ref, wt_ref, pm_ref, pt_ref, xb_ref):
    @pl.when(pl.program_id(1) == 0)
    def _():
        xb = x_ref[...].astype(BF16)
        xb_ref[...] = xb
        pt_ref[...] = jnp.dot(xb, wt_ref[0], preferred_element_type=F32)

    pm_ref[...] = jnp.dot(xb_ref[...], wm_ref[0], preferred_element_type=F32)


def _project(x2d, w_main, w_tail, layer, tm):
    n = x2d.shape[0]
    tn = 512
    return pl.pallas_call(
        _proj_body,
        grid=(n // tm, MAIN_W // tn),
        in_specs=[
            pl.BlockSpec((tm, D_MODEL), lambda i, j: (i, 0)),
            pl.BlockSpec((1, D_MODEL, tn), lambda i, j: (layer, 0, j)),
            pl.BlockSpec((1, D_MODEL, TAIL_W), lambda i, j: (layer, 0, 0)),
        ],
        out_specs=[
            pl.BlockSpec((tm, tn), lambda i, j: (i, j)),
            pl.BlockSpec((tm, TAIL_W), lambda i, j: (i, 0)),
        ],
        out_shape=[jax.ShapeDtypeStruct((n, MAIN_W), F32), jax.ShapeDtypeStruct((n, TAIL_W), F32)],
        scratch_shapes=[pltpu.VMEM((tm, D_MODEL), BF16)],
        compiler_params=_cparams(("arbitrary", "arbitrary")),
        name="proj",
    )(x2d, w_main, w_tail)


def _ret_body(q_ref, k_ref, v_ref, cos_ref, sin_ref, s0_ref, o_ref, sout_ref, st_ref, *, C, NC):
    c = pl.program_id(1)
    CP = max(C, LANES)

    @pl.when(c == 0)
    def _():
        st_ref[...] = s0_ref[0]

    def pad(a):
        if CP == C:
            return a
        return jnp.concatenate([a, jnp.zeros((CP - C, a.shape[1]), a.dtype)], axis=0)

    cosf = pad(cos_ref[...])
    sinf = pad(sin_ref[...])
    ii = lax.broadcasted_iota(I32, (CP, CP), 0)
    jj = lax.broadcasted_iota(I32, (CP, CP), 1)
    diff = (ii - jj).astype(F32)
    pos = lax.broadcasted_iota(I32, (CP, 1), 0).astype(F32)
    for h, lg in enumerate(_log_gammas()):
        sl = slice(h * RET_D, (h + 1) * RET_D)
        q = pad(q_ref[0, :, sl])
        k = pad(k_ref[0, :, sl])
        vb = pad(v_ref[0, :, sl]).astype(BF16)
        q = q * cosf + pltpu.roll(q, RET_D // 2, 1) * sinf
        k = (k * cosf + pltpu.roll(k, RET_D // 2, 1) * sinf) * (RET_D ** -0.5)
        decay = jnp.where(diff >= 0, jnp.exp(lg * jnp.maximum(diff, 0.0)), 0.0)
        qb = q.astype(BF16)
        scores = lax.dot_general(qb, k.astype(BF16), (((1,), (1,)), ((), ())),
                                 preferred_element_type=F32) * decay
        inner = jnp.dot(scores.astype(BF16), vb, preferred_element_type=F32)
        st = st_ref[h]
        cross = jnp.dot(qb, st.astype(BF16), preferred_element_type=F32) * jnp.exp(lg * (pos + 1.0))
        o_ref[0, :, sl] = (inner + cross)[:C]
        kd = k * jnp.exp(lg * (C - 1.0 - pos))
        st_ref[h] = st * float(np.exp(np.float32(lg) * np.float32(C))) + jnp.dot(
            kd.T.astype(BF16), vb, preferred_element_type=F32)

    @pl.when(c == NC - 1)
    def _():
        sout_ref[0] = st_ref[...]


def _retention(pm3, cosf, sinf, state0, C):
    B, T, _ = pm3.shape
    NC = T // C
    W = RET_HEADS * RET_D
    return pl.pallas_call(
        functools.partial(_ret_body, C=C, NC=NC),
        grid=(B, NC),
        in_specs=[
            pl.BlockSpec((1, C, W), lambda b, c: (b, c, 0)),
            pl.BlockSpec((1, C, W), lambda b, c: (b, c, 1)),
            pl.BlockSpec((1, C, W), lambda b, c: (b, c, 2)),
            pl.BlockSpec((C, RET_D), lambda b, c: (c, 0)),
            pl.BlockSpec((C, RET_D), lambda b, c: (c, 0)),
            pl.BlockSpec((1, RET_HEADS, RET_D, RET_D), lambda b, c: (b, 0, 0, 0)),
        ],
        out_specs=[
            pl.BlockSpec((1, C, W), lambda b, c: (b, c, 0)),
            pl.BlockSpec((1, RET_HEADS, RET_D, RET_D), lambda b, c: (b, 0, 0, 0)),
        ],
        out_shape=[jax.ShapeDtypeStruct((B, T, W), F32),
                   jax.ShapeDtypeStruct((B, RET_HEADS, RET_D, RET_D), F32)],
        scratch_shapes=[pltpu.VMEM((RET_HEADS, RET_D, RET_D), F32)],
        compiler_params=_cparams(("arbitrary", "arbitrary")),
        name="retention",
    )(pm3, pm3, pm3, cosf, sinf, state0)


def _sortable(x):
    bits = pltpu.bitcast(x + 0.0, I32)
    return bits ^ ((bits >> 31) & 0x7FFFFFFF)


def _kth_largest(count_ge, shape, kf):
    c0 = count_ge(jnp.zeros(shape, I32))
    tau = jnp.where(c0 >= kf, 0, INT_MIN).astype(I32)

    def body(it, tau):
        cand = tau | (jnp.int32(1) << (30 - it))
        return jnp.where(count_ge(cand) >= kf, cand, tau)

    return lax.fori_loop(0, 31, body, tau)


def _tie_cutoff(count_lt, shape, need, nbits):
    def body(it, m):
        cand = m + (jnp.int32(1) << (nbits - 1 - it))
        return jnp.where(count_lt(cand) < need, cand, m)

    return lax.fori_loop(0, nbits, body, jnp.zeros(shape, I32))


def _col_reduce(x, op):
    S, Q = x.shape
    if S > LANES:
        x = op(x.reshape(S // LANES, LANES, Q), axis=0)
    return op(x, axis=0, keepdims=True)


def _dsa_prompt_block(i, S, qa_ref, qi_ref, ptq_ref, o_ref, kb_ref, vt_ref, kib_ref, keys_ref, bias_ref,
                      *, QB, ksel):
    RC = 256 if S % 256 == 0 else LANES
    qiT = qi_ref[0].T
    ptT = ptq_ref[0].T
    zeros64 = jnp.zeros((IDX_DIM, QB), BF16)
    rhs_idx = jnp.concatenate(
        [jnp.concatenate([qiT[h * IDX_DIM:(h + 1) * IDX_DIM].astype(BF16), zeros64], axis=0)
         for h in range(IDX_HEADS)], axis=1)
    w_rows = [ptT[IDX_DIM + h:IDX_DIM + h + 1, :] * (IDX_HEADS ** -0.5) * (IDX_DIM ** -0.5)
              for h in range(IDX_HEADS)]
    t_pos = i * QB + lax.broadcasted_iota(I32, (RC, QB), 1)
    for r in range(S // RC):
        d = jnp.dot(kib_ref[r * RC:(r + 1) * RC, :], rhs_idx, preferred_element_type=F32)
        acc = jnp.maximum(d[:, 0:QB], 0.0) * w_rows[0]
        for h in range(1, IDX_HEADS):
            acc = acc + jnp.maximum(d[:, h * QB:(h + 1) * QB], 0.0) * w_rows[h]
        s_pos = r * RC + lax.broadcasted_iota(I32, (RC, QB), 0)
        keys_ref[r * RC:(r + 1) * RC, :] = _sortable(jnp.where(s_pos <= t_pos, acc, NEG_INF))

    kf = float(ksel)
    NT = S // LANES

    def count(pred):
        acc = jnp.zeros((LANES, QB), F32)
        for r in range(NT):
            acc = jnp.where(pred(keys_ref[r * LANES:(r + 1) * LANES, :], r * LANES), acc + 1.0, acc)
        return jnp.sum(acc, axis=0, keepdims=True)

    tau = _kth_largest(lambda cand: count(lambda k, _: k >= cand), (1, QB), kf)
    row = lax.broadcasted_iota(I32, (LANES, QB), 0)
    t_row = i * QB + lax.broadcasted_iota(I32, (LANES, QB), 1)
    need = kf - count(lambda k, _: k > tau)
    n_eq = count(lambda k, r0: (k == tau) & (r0 + row <= t_row))
    any_split = jnp.max(jnp.where(n_eq > need, 1.0, 0.0)) > 0.0

    def slow():
        return _tie_cutoff(lambda m: count(lambda k, r0: (k == tau) & (r0 + row <= t_row) & (r0 + row < m)),
                           (1, QB), need, int(S).bit_length())

    mcut = lax.cond(any_split, slow, lambda: jnp.full((1, QB), S, I32))
    for r in range(NT):
        k = keys_ref[r * LANES:(r + 1) * LANES, :]
        s_pos = r * LANES + row
        sel = (s_pos <= t_row) & ((k > tau) | ((k == tau) & (s_pos <= mcut)))
        bias_ref[r * LANES:(r + 1) * LANES, :] = jnp.where(sel, 0.0, NEG_INF)

    qaT = qa_ref[0].T * (ATT_HD ** -0.5)
    zq = jnp.zeros((ATT_HD, QB), BF16)
    outs = []
    for j in range(ATT_HEADS // 2):
        r0 = 2 * j * ATT_HD
        top = jnp.concatenate([qaT[r0:r0 + ATT_HD].astype(BF16), zq], axis=1)
        bot = jnp.concatenate([zq, qaT[r0 + ATT_HD:r0 + 2 * ATT_HD].astype(BF16)], axis=1)
        rhs = jnp.concatenate([top, bot], axis=0)
        sT = jnp.dot(kb_ref[0:S, r0:r0 + 2 * ATT_HD], rhs, preferred_element_type=F32)
        for u in range(2):
            h = 2 * j + u
            s = sT[:, u * QB:(u + 1) * QB] + bias_ref[0:S, :]
            m = _col_reduce(s, jnp.max)
            p = jnp.exp(s - m)
            l = _col_reduce(p, jnp.sum)
            oT = jnp.dot(vt_ref[h * ATT_HD:(h + 1) * ATT_HD, 0:S], p.astype(BF16),
                         preferred_element_type=F32)
            outs.append(oT / l)
    o_ref[0] = jnp.concatenate(outs, axis=0).T


def _dsa_prompt_body(qa_ref, k_ref, v_ref, qi_ref, pt_ref, ptq_ref, o_ref,
                     kb_ref, vt_ref, kib_ref, keys_ref, bias_ref, *, T, QB, ksel, NV):
    i = pl.program_id(1)

    @pl.when(i == 0)
    def _():
        kb_ref[...] = k_ref[0].astype(BF16)
        vt_ref[...] = v_ref[0].T.astype(BF16)
        kib_ref[...] = pt_ref[0].astype(BF16)

    per = (T // QB) // NV
    for g in range(NV):
        @pl.when(i // per == g)
        def _(g=g):
            _dsa_prompt_block(i, (g + 1) * per * QB, qa_ref, qi_ref, ptq_ref, o_ref, kb_ref, vt_ref, kib_ref,
                              keys_ref, bias_ref, QB=QB, ksel=ksel)


def _dsa_prompt(pm3, pt3):
    B, T, _ = pm3.shape
    QB = min(LANES, T)
    ksel = min(TOPK_MAX, T // 4)
    W = ATT_HEADS * ATT_HD
    NV = min(4, T // QB)
    return pl.pallas_call(
        functools.partial(_dsa_prompt_body, T=T, QB=QB, ksel=ksel, NV=NV),
        grid=(B, T // QB),
        in_specs=[
            pl.BlockSpec((1, QB, W), lambda b, i: (b, i, 4)),
            pl.BlockSpec((1, T, W), lambda b, i: (b, 0, 5)),
            pl.BlockSpec((1, T, W), lambda b, i: (b, 0, 6)),
            pl.BlockSpec((1, QB, W), lambda b, i: (b, i, 7)),
            pl.BlockSpec((1, T, TAIL_W), lambda b, i: (b, 0, 0)),
            pl.BlockSpec((1, QB, TAIL_W), lambda b, i: (b, i, 0)),
        ],
        out_specs=pl.BlockSpec((1, QB, W), lambda b, i: (b, i, 0)),
        out_shape=jax.ShapeDtypeStruct((B, T, W), F32),
        scratch_shapes=[pltpu.VMEM((T, W), BF16), pltpu.VMEM((W, T), BF16),
                        pltpu.VMEM((T, TAIL_W), BF16), pltpu.VMEM((T, QB), I32),
                        pltpu.VMEM((T, QB), F32)],
        compiler_params=_cparams(("arbitrary", "arbitrary")),
        name="dsa_prompt",
    )(pm3, pm3, pm3, pm3, pt3, pt3)


PAGES_PER_STEP = 16


def _idx_sample_body(ptab_ref, *refs, P):
    del ptab_ref
    page_refs = refs[:P]
    qi_ref, w_ref, kin_ref, sc_ref, scn_ref = refs[P:]
    q = qi_ref[0].astype(BF16)
    w = w_ref[0] * (IDX_HEADS ** -0.5) * (IDX_DIM ** -0.5)
    nq = q.shape[0] // IDX_HEADS

    def scores(keys_t):
        d = jnp.dot(q, keys_t.astype(BF16), preferred_element_type=F32)
        r = jnp.maximum(d, 0.0) * w
        acc = r[0:nq]
        for h in range(1, IDX_HEADS):
            acc = acc + r[h * nq:(h + 1) * nq]
        return acc

    sc_ref[0] = scores(jnp.concatenate([r[0, 0] for r in page_refs], axis=1))

    @pl.when(pl.program_id(1) == 0)
    def _():
        scn_ref[0] = scores(kin_ref[0])


def _idx_sample(page_table, cache_ki_t, layer, qi_flat, w_col, ki_new_t):
    DB, n_pages = page_table.shape
    P = PAGES_PER_STEP
    NCH = n_pages // P
    nq = qi_flat.shape[1] // IDX_HEADS

    def page_spec(p):
        return pl.BlockSpec((1, 1, IDX_DIM, PAGE), lambda b, c, pt: (layer, pt[b, c * P + p], 0, 0))

    grid_spec = pltpu.PrefetchScalarGridSpec(
        num_scalar_prefetch=1,
        grid=(DB, NCH),
        in_specs=[page_spec(p) for p in range(P)] + [
            pl.BlockSpec((1, IDX_HEADS * nq, IDX_DIM), lambda b, c, pt: (b, 0, 0)),
            pl.BlockSpec((1, IDX_HEADS * nq, 1), lambda b, c, pt: (b, 0, 0)),
            pl.BlockSpec((1, IDX_DIM, PAGE), lambda b, c, pt: (b, 0, 0)),
        ],
        out_specs=[
            pl.BlockSpec((1, nq, P * PAGE), lambda b, c, pt: (b, 0, c)),
            pl.BlockSpec((1, nq, PAGE), lambda b, c, pt: (b, 0, 0)),
        ],
    )
    return pl.pallas_call(
        functools.partial(_idx_sample_body, P=P),
        grid_spec=grid_spec,
        out_shape=[jax.ShapeDtypeStruct((DB, nq, NCH * P * PAGE), F32),
                   jax.ShapeDtypeStruct((DB, nq, PAGE), F32)],
        compiler_params=_cparams(("arbitrary", "arbitrary")),
        name="idx_sample",
    )(page_table, *([cache_ki_t] * P), qi_flat, w_col, ki_new_t)


def _select_sample_body(sc_ref, scn_ref, bm_ref, bn_ref, keys_ref, *, GB, nq, past, ksel):
    R = GB * nq
    S = past + PAGE
    NT = S // LANES
    t_of_row = lax.broadcasted_iota(I32, (GB, nq, LANES), 1).reshape(R, LANES)
    lane = lax.broadcasted_iota(I32, (R, LANES), 1)
    new_valid = lane <= t_of_row
    keys_ref[:, :past] = _sortable(sc_ref[...].reshape(R, past))
    keys_ref[:, past:] = _sortable(jnp.where(new_valid, scn_ref[...].reshape(R, PAGE), NEG_INF))
    kf = float(ksel)

    def count(pred):
        acc = jnp.zeros((R, LANES), F32)
        for c in range(NT):
            ok = pred(keys_ref[:, c * LANES:(c + 1) * LANES], c * LANES, new_valid if c == NT - 1 else None)
            acc = jnp.where(ok, acc + 1.0, acc)
        return jnp.sum(acc, axis=1, keepdims=True)

    def tied(k, valid, tau):
        return (k == tau) if valid is None else (k == tau) & valid

    tau = _kth_largest(lambda cand: count(lambda k, c0, v: k >= cand), (R, 1), kf)
    need = kf - count(lambda k, c0, v: k > tau)
    n_eq = count(lambda k, c0, v: tied(k, v, tau))
    any_split = jnp.max(jnp.where(n_eq > need, 1.0, 0.0)) > 0.0

    def slow():
        return _tie_cutoff(lambda m: count(lambda k, c0, v: tied(k, v, tau) & (c0 + lane < m)),
                           (R, 1), need, int(S).bit_length())

    mcut = lax.cond(any_split, slow, lambda: jnp.full((R, 1), S, I32))
    for c in range(NT):
        k = keys_ref[:, c * LANES:(c + 1) * LANES]
        valid = new_valid if c == NT - 1 else None
        sel = (k > tau) | (tied(k, valid, tau) & (c * LANES + lane <= mcut))
        if valid is not None:
            sel = sel & valid
        bias = jnp.where(sel, 0.0, NEG_INF).reshape(GB, nq, LANES)
        if c == NT - 1:
            bn_ref[...] = bias
        else:
            bm_ref[:, :, c * LANES:(c + 1) * LANES] = bias


def _select_sample(sc, scn, ksel):
    DB, nq, past = sc.shape
    GB = int(np.gcd(DB, 8))
    return pl.pallas_call(
        functools.partial(_select_sample_body, GB=GB, nq=nq, past=past, ksel=ksel),
        grid=(DB // GB,),
        in_specs=[pl.BlockSpec((GB, nq, past), lambda b: (b, 0, 0)),
                  pl.BlockSpec((GB, nq, PAGE), lambda b: (b, 0, 0))],
        out_specs=[pl.BlockSpec((GB, nq, past), lambda b: (b, 0, 0)),
                   pl.BlockSpec((GB, nq, PAGE), lambda b: (b, 0, 0))],
        out_shape=[jax.ShapeDtypeStruct((DB, nq, past), F32),
                   jax.ShapeDtypeStruct((DB, nq, PAGE), F32)],
        scratch_shapes=[pltpu.VMEM((GB * nq, past + PAGE), I32)],
        compiler_params=_cparams(("arbitrary",)),
        name="select_sample",
    )(sc, scn)


def _att_sample_body(ptab_ref, *refs, P, NCH, nq):
    del ptab_ref
    k_refs = refs[:P]
    v_refs = refs[P:2 * P]
    q_ref, bm_ref, bn_ref, kn_ref, vn_ref, o_ref, m_ref, l_ref, acc_ref = refs[2 * P:]
    c = pl.program_id(1)

    @pl.when(c == 0)
    def _():
        m_ref[...] = jnp.full(m_ref.shape, NEG_INF, F32)
        l_ref[...] = jnp.zeros(l_ref.shape, F32)
        acc_ref[...] = jnp.zeros(acc_ref.shape, F32)

    q = (q_ref[0] * (ATT_HD ** -0.5)).astype(BF16)

    def update(k_t, v_t, bias):
        s = jnp.dot(q, k_t.astype(BF16), preferred_element_type=F32)
        s = s + jnp.concatenate([bias] * ATT_HEADS, axis=0)
        m_old = m_ref[...]
        m_new = jnp.maximum(m_old, jnp.max(s, axis=1, keepdims=True))
        m_safe = jnp.where(m_new == NEG_INF, 0.0, m_new)
        alpha = jnp.exp(m_old - m_safe)
        p = jnp.exp(s - m_safe)
        l_ref[...] = l_ref[...] * alpha + jnp.sum(p, axis=1, keepdims=True)
        acc_ref[...] = acc_ref[...] * alpha + lax.dot_general(
            p.astype(BF16), v_t.astype(BF16), (((1,), (1,)), ((), ())), preferred_element_type=F32)
        m_ref[...] = m_new

    update(jnp.concatenate([r[0, 0] for r in k_refs], axis=1),
           jnp.concatenate([r[0, 0] for r in v_refs], axis=1), bm_ref[0])

    @pl.when(c == NCH - 1)
    def _():
        update(kn_ref[0], vn_ref[0], bn_ref[0])
        o = acc_ref[...] / l_ref[...]
        head_of_lane = lax.broadcasted_iota(I32, (nq, ATT_HEADS * ATT_HD), 1) // ATT_HD
        out = jnp.zeros((nq, ATT_HEADS * ATT_HD), F32)
        for h in range(ATT_HEADS):
            out = out + jnp.where(head_of_lane == h, o[h * nq:(h + 1) * nq], 0.0)
        o_ref[0] = out


def _att_sample(page_table, cache_k_t, cache_v_t, layer, q_bd, bias_main, bias_new, k_new_t, v_new_t):
    DB, n_pages = page_table.shape
    P = PAGES_PER_STEP
    NCH = n_pages // P
    nq = bias_main.shape[1]
    W = ATT_HEADS * ATT_HD

    def page_spec(p):
        return pl.BlockSpec((1, 1, W, PAGE), lambda b, c, pt: (layer, pt[b, c * P + p], 0, 0))

    grid_spec = pltpu.PrefetchScalarGridSpec(
        num_scalar_prefetch=1,
        grid=(DB, NCH),
        in_specs=[page_spec(p) for p in range(P)] + [page_spec(p) for p in range(P)] + [
            pl.BlockSpec((1, ATT_HEADS * nq, W), lambda b, c, pt: (b, 0, 0)),
            pl.BlockSpec((1, nq, P * PAGE), lambda b, c, pt: (b, 0, c)),
            pl.BlockSpec((1, nq, PAGE), lambda b, c, pt: (b, 0, 0)),
            pl.BlockSpec((1, W, PAGE), lambda b, c, pt: (b, 0, 0)),
            pl.BlockSpec((1, W, PAGE), lambda b, c, pt: (b, 0, 0)),
        ],
        out_specs=pl.BlockSpec((1, nq, W), lambda b, c, pt: (b, 0, 0)),
        scratch_shapes=[pltpu.VMEM((ATT_HEADS * nq, 1), F32), pltpu.VMEM((ATT_HEADS * nq, 1), F32),
                        pltpu.VMEM((ATT_HEADS * nq, W), F32)],
    )
    return pl.pallas_call(
        functools.partial(_att_sample_body, P=P, NCH=NCH, nq=nq),
        grid_spec=grid_spec,
        out_shape=jax.ShapeDtypeStruct((DB, nq, W), F32),
        compiler_params=_cparams(("arbitrary", "arbitrary")),
        name="att_sample",
    )(page_table, *([cache_k_t] * P), *([cache_v_t] * P), q_bd, bias_main, bias_new, k_new_t, v_new_t)


def _dsa_sample(pm3, pt3, cache_k_t, cache_v_t, cache_ki_t, page_table, layer):
    DB, nq, _ = pm3.shape
    W = ATT_HEADS * ATT_HD
    past = page_table.shape[1] * PAGE
    ksel = min(TOPK_MAX, (past + nq) // 4)
    q_a = pm3[:, :, 4 * W:5 * W].reshape(DB, nq, ATT_HEADS, ATT_HD)
    q_i = pm3[:, :, 7 * W:8 * W].reshape(DB, nq, IDX_HEADS, IDX_DIM)
    qi_flat = q_i.transpose(0, 2, 1, 3).reshape(DB, IDX_HEADS * nq, IDX_DIM)
    w_col = pt3[:, :, IDX_DIM:IDX_DIM + IDX_HEADS].transpose(0, 2, 1).reshape(DB, IDX_HEADS * nq, 1)
    q_bd = jnp.einsum("bthd,hg->bhtgd", q_a, jnp.eye(ATT_HEADS, dtype=F32)).reshape(DB, ATT_HEADS * nq, W)

    def new_t(a):
        return jnp.pad(a.transpose(0, 2, 1), ((0, 0), (0, 0), (0, PAGE - nq)))

    sc, scn = _idx_sample(page_table, cache_ki_t, layer, qi_flat, w_col, new_t(pt3[:, :, :IDX_DIM]))
    bias_main, bias_new = _select_sample(sc, scn, ksel)
    return _att_sample(page_table, cache_k_t, cache_v_t, layer, q_bd, bias_main, bias_new,
                       new_t(pm3[:, :, 5 * W:6 * W]), new_t(pm3[:, :, 6 * W:7 * W]))


def _layer_norm(y, g, b):
    mu = jnp.mean(y, axis=-1, keepdims=True)
    var = jnp.mean(jnp.square(y - mu), axis=-1, keepdims=True)
    return (y - mu) * lax.rsqrt(var + LN_EPS) * g + b


def _top2_of4(b0, b1, b2, b3):
    hi01, lo01 = jnp.maximum(b0, b1), jnp.minimum(b0, b1)
    hi23, lo23 = jnp.maximum(b2, b3), jnp.minimum(b2, b3)
    return jnp.maximum(hi01, hi23), jnp.maximum(jnp.minimum(hi01, hi23), jnp.maximum(lo01, lo23))


def _route_rows(aff, biased):
    gscore = []
    for g in range(N_GROUPS):
        m1, m2 = _top2_of4(*biased[g * GROUP_SIZE:(g + 1) * GROUP_SIZE])
        gscore.append(m1 + m2)
    gmax = functools.reduce(jnp.maximum, gscore)
    taken = jnp.zeros_like(gmax, dtype=jnp.bool_)
    gsel = []
    for g in range(N_GROUPS):
        hit = (gscore[g] == gmax) & jnp.logical_not(taken)
        gsel.append(hit)
        taken = taken | hit
    cand, caff = [], []
    for k in range(GROUP_SIZE):
        c = jnp.full_like(gmax, NEG_INF)
        a = jnp.zeros_like(gmax)
        for g in range(N_GROUPS):
            c = jnp.where(gsel[g], biased[g * GROUP_SIZE + k], c)
            a = jnp.where(gsel[g], aff[g * GROUP_SIZE + k], a)
        cand.append(c)
        caff.append(a)
    best = functools.reduce(jnp.maximum, cand)
    taken = jnp.zeros_like(taken)
    first = []
    for k in range(GROUP_SIZE):
        hit = (cand[k] == best) & jnp.logical_not(taken)
        first.append(hit)
        taken = taken | hit
    rest = [jnp.where(first[k], NEG_INF, cand[k]) for k in range(GROUP_SIZE)]
    best2 = functools.reduce(jnp.maximum, rest)
    taken = jnp.zeros_like(taken)
    second = []
    for k in range(GROUP_SIZE):
        hit = (rest[k] == best2) & jnp.logical_not(first[k]) & jnp.logical_not(taken)
        second.append(hit)
        taken = taken | hit
    a1 = functools.reduce(jnp.add, [jnp.where(first[k], caff[k], 0.0) for k in range(GROUP_SIZE)])
    a2 = functools.reduce(jnp.add, [jnp.where(second[k], caff[k], 0.0) for k in range(GROUP_SIZE)])
    tot = a1 + a2
    w1, w2 = a1 / tot, a2 / tot
    gates = []
    for g in range(N_GROUPS):
        for k in range(GROUP_SIZE):
            gates.append(jnp.where(gsel[g] & first[k], w1, jnp.where(gsel[g] & second[k], w2, 0.0)))
    return gates


def _merge_body(oret_ref, g_ref, oatt_ref, x_ref, gn_ref, wout_ref, lg_ref, lb_ref, rwt_ref, rb_ref,
                x1_ref, gates_ref, wob_ref, *, alpha):
    @pl.when(pl.program_id(0) == 0)
    def _():
        wob_ref[...] = wout_ref[0].astype(BF16)

    parts = []
    for h in range(RET_HEADS):
        sl = slice(h * RET_D, (h + 1) * RET_D)
        o = oret_ref[:, sl]
        mu = jnp.mean(o, axis=-1, keepdims=True)
        var = jnp.mean(jnp.square(o - mu), axis=-1, keepdims=True)
        r = (o - mu) * lax.rsqrt(var + LN_EPS) * gn_ref[0, :, sl]
        g = g_ref[:, sl]
        parts.append(g * jax.nn.sigmoid(g) * r)
    cat = jnp.concatenate(parts + [oatt_ref[...]], axis=1).astype(BF16)
    mix = jnp.dot(cat, wob_ref[...], preferred_element_type=F32)
    x1 = _layer_norm(alpha * x_ref[...] + mix, lg_ref[0], lb_ref[0])
    x1_ref[...] = x1

    logits = lax.dot_general(rwt_ref[...], x1, (((1,), (1,)), ((), ())),
                             precision=lax.Precision.HIGHEST, preferred_element_type=F32)
    aff = jax.nn.sigmoid(logits)
    biased = aff + rb_ref[...]
    gates = _route_rows([aff[e:e + 1] for e in range(N_EXPERTS)],
                        [biased[e:e + 1] for e in range(N_EXPERTS)])
    tm = x1.shape[0]
    gt = jnp.concatenate(gates + [jnp.zeros((LANES - N_EXPERTS, tm), F32)], axis=0)
    gates_ref[...] = gt.T


def _merge(o_ret, pm, o_att, x2d, gn_w, w_out, ln_g, ln_b, rwt, rb_col, layer, tm, alpha):
    n = x2d.shape[0]
    W = RET_HEADS * RET_D
    vec = lambda width: pl.BlockSpec((1, 1, width), lambda i: (layer, 0, 0))
    return pl.pallas_call(
        functools.partial(_merge_body, alpha=alpha),
        grid=(n // tm,),
        in_specs=[
            pl.BlockSpec((tm, W), lambda i: (i, 0)),
            pl.BlockSpec((tm, W), lambda i: (i, 3)),
            pl.BlockSpec((tm, W), lambda i: (i, 0)),
            pl.BlockSpec((tm, D_MODEL), lambda i: (i, 0)),
            vec(W),
            pl.BlockSpec((1, D_MODEL, D_MODEL), lambda i: (layer, 0, 0)),
            vec(D_MODEL), vec(D_MODEL),
            pl.BlockSpec((N_EXPERTS, D_MODEL), lambda i: (0, 0)),
            pl.BlockSpec((N_EXPERTS, 1), lambda i: (0, 0)),
        ],
        out_specs=[pl.BlockSpec((tm, D_MODEL), lambda i: (i, 0)),
                   pl.BlockSpec((tm, LANES), lambda i: (i, 0))],
        out_shape=[jax.ShapeDtypeStruct((n, D_MODEL), F32), jax.ShapeDtypeStruct((n, LANES), F32)],
        scratch_shapes=[pltpu.VMEM((D_MODEL, D_MODEL), BF16)],
        compiler_params=_cparams(("arbitrary",)),
        name="merge",
    )(o_ret, pm, o_att, x2d, gn_w, w_out, ln_g, ln_b, rwt, rb_col)


def _moe_body(x1_ref, gates_ref, wg_ref, wu_ref, wd_ref, lg_ref, lb_ref, out_ref, xb_ref, *, alpha):
    e = pl.program_id(1)

    @pl.when(e == 0)
    def _():
        xb_ref[...] = x1_ref[...].astype(BF16)
        out_ref[...] = jnp.zeros(out_ref.shape, F32)

    xb = xb_ref[...]
    a = jnp.dot(xb, wg_ref[0, 0].astype(BF16), preferred_element_type=F32)
    u = jnp.dot(xb, wu_ref[0, 0].astype(BF16), preferred_element_type=F32)
    hid = (a * jax.nn.sigmoid(a) * u).astype(BF16)
    y = jnp.dot(hid, wd_ref[0, 0].astype(BF16), preferred_element_type=F32)
    gates = gates_ref[...]
    lane = lax.broadcasted_iota(I32, gates.shape, 1)
    g = jnp.sum(jnp.where(lane == e, gates, 0.0), axis=1, keepdims=True)
    out_ref[...] += g * y

    @pl.when(e == N_EXPERTS - 1)
    def _():
        out_ref[...] = _layer_norm(alpha * x1_ref[...] + out_ref[...], lg_ref[0], lb_ref[0])


def _moe(x1, gates, w_gate, w_up, w_down, ln_g, ln_b, layer, tm, alpha):
    n = x1.shape[0]
    dff = w_gate.shape[-1]
    vec = pl.BlockSpec((1, 1, D_MODEL), lambda i, e: (layer, 0, 0))
    return pl.pallas_call(
        functools.partial(_moe_body, alpha=alpha),
        grid=(n // tm, N_EXPERTS),
        in_specs=[
            pl.BlockSpec((tm, D_MODEL), lambda i, e: (i, 0)),
            pl.BlockSpec((tm, LANES), lambda i, e: (i, 0)),
            pl.BlockSpec((1, 1, D_MODEL, dff), lambda i, e: (layer, e, 0, 0)),
            pl.BlockSpec((1, 1, D_MODEL, dff), lambda i, e: (layer, e, 0, 0)),
            pl.BlockSpec((1, 1, dff, D_MODEL), lambda i, e: (layer, e, 0, 0)),
            vec, vec,
        ],
        out_specs=pl.BlockSpec((tm, D_MODEL), lambda i, e: (i, 0)),
        out_shape=jax.ShapeDtypeStruct((n, D_MODEL), F32),
        scratch_shapes=[pltpu.VMEM((tm, D_MODEL), BF16)],
        compiler_params=_cparams(("arbitrary", "arbitrary")),
        name="moe",
    )(x1, gates, w_gate, w_up, w_down, ln_g, ln_b)


def _rope_tables(pos):
    half = RET_D // 2
    inv = ROPE_BASE ** (-jnp.arange(half, dtype=F32) / half)
    ang = pos.astype(F32)[:, None] * inv[None, :]
    cos, sin = jnp.cos(ang), jnp.sin(ang)
    return jnp.concatenate([cos, cos], axis=1), jnp.concatenate([-sin, sin], axis=1)


def kernel(x_prompt, x_sample, cache_k, cache_v, cache_kidx, state_ret, page_table, w_in, ret_gn_w, w_out,
           ln1_g, ln1_b, router_w, router_b, w_gate, w_up, w_down, ln2_g, ln2_b):
    depth = w_in.shape[0]
    B, T, D = x_prompt.shape
    DB, TS, _ = x_sample.shape
    n_pool = cache_k.shape[1]
    W = ATT_HEADS * ATT_HD
    past = page_table.shape[1] * PAGE
    alpha = (2 * depth) ** 0.25

    w_main = w_in[:, :, :MAIN_W].astype(BF16)
    w_tail = jnp.pad(w_in[:, :, MAIN_W:], ((0, 0), (0, 0), (0, TAIL_W - (w_in.shape[2] - MAIN_W)))).astype(BF16)
    cache_k_t = cache_k.transpose(0, 1, 3, 4, 2).reshape(depth, n_pool, W, PAGE)
    cache_v_t = cache_v.transpose(0, 1, 3, 4, 2).reshape(depth, n_pool, W, PAGE)
    cache_ki_t = cache_kidx.transpose(0, 1, 3, 2)
    rwt = router_w.T
    rb_col = router_b.reshape(N_EXPERTS, 1)
    vec3 = lambda a: a.reshape(depth, 1, a.shape[-1])
    gn3, l1g, l1b, l2g, l2b = vec3(ret_gn_w), vec3(ln1_g), vec3(ln1_b), vec3(ln2_g), vec3(ln2_b)
    cos_p, sin_p = _rope_tables(jnp.arange(T))
    cos_s, sin_s = _rope_tables(past + jnp.arange(TS))
    zero_state = jnp.zeros((B, RET_HEADS, RET_D, RET_D), F32)
    C = min(LANES, T)

    def block(x2d, nb, nt, o_att_fn, cosf, sinf, state0, chunk, tm_proj, tm_merge, tm_moe, l):
        pm, pt = _project(x2d, w_main, w_tail, l, tm_proj)
        pm3 = pm.reshape(nb, nt, MAIN_W)
        pt3 = pt.reshape(nb, nt, TAIL_W)
        o_ret, st = _retention(pm3, cosf, sinf, state0, chunk)
        o_att = o_att_fn(pm3, pt3)
        x1, gates = _merge(o_ret.reshape(nb * nt, -1), pm, o_att.reshape(nb * nt, -1), x2d, gn3, w_out,
                           l1g, l1b, rwt, rb_col, l, tm_merge, alpha)
        x2 = _moe(x1, gates, w_gate, w_up, w_down, l2g, l2b, l, tm_moe, alpha)
        k_a = pm3[:, :, 5 * W:6 * W].reshape(nb, nt, ATT_HEADS, ATT_HD)
        v_a = pm3[:, :, 6 * W:7 * W].reshape(nb, nt, ATT_HEADS, ATT_HD)
        k_i = pt3[:, :, :IDX_DIM]
        return x2, k_a, v_a, k_i, st

    xp = x_prompt.reshape(B * T, D)
    xs = x_sample.reshape(DB * TS, D)
    outs_p, outs_s = [], []
    for l in range(depth):
        xp, *rest = block(xp, B, T, _dsa_prompt, cos_p, sin_p, zero_state, C,
                          min(2048, B * T), min(512, B * T), min(1024, B * T), l)
        outs_p.append(rest)
        sample_att = functools.partial(_dsa_sample, cache_k_t=cache_k_t, cache_v_t=cache_v_t,
                                       cache_ki_t=cache_ki_t, page_table=page_table, layer=l)
        xs, *rest = block(xs, DB, TS, sample_att, cos_s, sin_s, state_ret[l], TS,
                          DB * TS, DB * TS, DB * TS, l)
        outs_s.append(rest)
    stack = lambda outs, i: jnp.stack([o[i] for o in outs])
    return (xp.reshape(B, T, D), xs.reshape(DB, TS, D),
            stack(outs_p, 0), stack(outs_p, 1), stack(outs_p, 2), stack(outs_p, 3),
            stack(outs_s, 0), stack(outs_s, 1), stack(outs_s, 2), stack(outs_s, 3))
```

```python
import functools

import numpy as np
import jax
import jax.numpy as jnp
from jax import lax
from jax.experimental import pallas as pl
from jax.experimental.pallas import tpu as pltpu

F32 = jnp.float32
BF16 = jnp.bfloat16
I32 = jnp.int32

RET_HEADS = 4
RET_D = 128
ATT_HEADS = 8
ATT_HD = 64
IDX_HEADS = 8
IDX_DIM = 64
TOPK_MAX = 256
PAGE = 128
N_EXPERTS = 16
N_GROUPS = 4
GROUP_SIZE = N_EXPERTS // N_GROUPS
ROPE_BASE = 10000.0
LN_EPS = 1e-5
D_MODEL = 1024
MAIN_W = 4096
TAIL_W = 128
LANES = 128
VMEM_LIMIT = 56 * 1024 * 1024

INT_MIN = -(2 ** 31)
NEG_INF = float("-inf")


def _log_gammas():
    h = np.arange(RET_HEADS, dtype=np.float32)
    return [float(v) for v in np.log1p(-np.exp2(-5.0 - h)).astype(np.float32)]


def _cparams(sem):
    return pltpu.CompilerParams(dimension_semantics=sem, vmem_limit_bytes=VMEM_LIMIT)


def _proj_body(x_ref, wm_ref, wt_ref, pm_ref, pt_ref, xb_ref):
    @pl.when(pl.program_id(1) == 0)
    def _():
        xb = x_ref[...].astype(BF16)
        xb_ref[...] = xb
        pt_ref[...] = jnp.dot(xb, wt_ref[0], preferred_element_type=F32)

    pm_ref[...] = jnp.dot(xb_ref[...], wm_ref[0], preferred_element_type=F32)


def _project(x2d, w_main, w_tail, layer, tm):
    n = x2d.shape[0]
    tn = 512
    return pl.pallas_call(
        _proj_body,
        grid=(n // tm, MAIN_W // tn),
        in_specs=[
            pl.BlockSpec((tm, D_MODEL), lambda i, j: (i, 0)),
            pl.BlockSpec((1, D_MODEL, tn), lambda i, j: (layer, 0, j)),
            pl.BlockSpec((1, D_MODEL, TAIL_W), lambda i, j: (layer, 0, 0)),
        ],
        out_specs=[
            pl.BlockSpec((tm, tn), lambda i, j: (i, j)),
            pl.BlockSpec((tm, TAIL_W), lambda i, j: (i, 0)),
        ],
        out_shape=[jax.ShapeDtypeStruct((n, MAIN_W), F32), jax.ShapeDtypeStruct((n, TAIL_W), F32)],
        scratch_shapes=[pltpu.VMEM((tm, D_MODEL), BF16)],
        compiler_params=_cparams(("arbitrary", "arbitrary")),
        name="proj",
    )(x2d, w_main, w_tail)


def _ret_body(q_ref, k_ref, v_ref, cos_ref, sin_ref, s0_ref, o_ref, sout_ref, st_ref, *, C, NC):
    c = pl.program_id(1)
    CP = max(C, LANES)

    @pl.when(c == 0)
    def _():
        st_ref[...] = s0_ref[0]

    def pad(a):
        if CP == C:
            return a
        return jnp.concatenate([a, jnp.zeros((CP - C, a.shape[1]), a.dtype)], axis=0)

    cosf = pad(cos_ref[...])
    sinf = pad(sin_ref[...])
    ii = lax.broadcasted_iota(I32, (CP, CP), 0)
    jj = lax.broadcasted_iota(I32, (CP, CP), 1)
    diff = (ii - jj).astype(F32)
    pos = lax.broadcasted_iota(I32, (CP, 1), 0).astype(F32)
    for h, lg in enumerate(_log_gammas()):
        sl = slice(h * RET_D, (h + 1) * RET_D)
        q = pad(q_ref[0, :, sl])
        k = pad(k_ref[0, :, sl])
        vb = pad(v_ref[0, :, sl]).astype(BF16)
        q = q * cosf + pltpu.roll(q, RET_D // 2, 1) * sinf
        k = (k * cosf + pltpu.roll(k, RET_D // 2, 1) * sinf) * (RET_D ** -0.5)
        decay = jnp.where(diff >= 0, jnp.exp(lg * jnp.maximum(diff, 0.0)), 0.0)
        qb = q.astype(BF16)
        scores = lax.dot_general(qb, k.astype(BF16), (((1,), (1,)), ((), ())),
                                 preferred_element_type=F32) * decay
        inner = jnp.dot(scores.astype(BF16), vb, preferred_element_type=F32)
        st = st_ref[h]
        cross = jnp.dot(qb, st.astype(BF16), preferred_element_type=F32) * jnp.exp(lg * (pos + 1.0))
        o_ref[0, :, sl] = (inner + cross)[:C]
        kd = k * jnp.exp(lg * (C - 1.0 - pos))
        st_ref[h] = st * float(np.exp(np.float32(lg) * np.float32(C))) + jnp.dot(
            kd.T.astype(BF16), vb, preferred_element_type=F32)

    @pl.when(c == NC - 1)
    def _():
        sout_ref[0] = st_ref[...]


def _retention(pm3, cosf, sinf, state0, C):
    B, T, _ = pm3.shape
    NC = T // C
    W = RET_HEADS * RET_D
    return pl.pallas_call(
        functools.partial(_ret_body, C=C, NC=NC),
        grid=(B, NC),
        in_specs=[
            pl.BlockSpec((1, C, W), lambda b, c: (b, c, 0)),
            pl.BlockSpec((1, C, W), lambda b, c: (b, c, 1)),
            pl.BlockSpec((1, C, W), lambda b, c: (b, c, 2)),
            pl.BlockSpec((C, RET_D), lambda b, c: (c, 0)),
            pl.BlockSpec((C, RET_D), lambda b, c: (c, 0)),
            pl.BlockSpec((1, RET_HEADS, RET_D, RET_D), lambda b, c: (b, 0, 0, 0)),
        ],
        out_specs=[
            pl.BlockSpec((1, C, W), lambda b, c: (b, c, 0)),
            pl.BlockSpec((1, RET_HEADS, RET_D, RET_D), lambda b, c: (b, 0, 0, 0)),
        ],
        out_shape=[jax.ShapeDtypeStruct((B, T, W), F32),
                   jax.ShapeDtypeStruct((B, RET_HEADS, RET_D, RET_D), F32)],
        scratch_shapes=[pltpu.VMEM((RET_HEADS, RET_D, RET_D), F32)],
        compiler_params=_cparams(("arbitrary", "arbitrary")),
        name="retention",
    )(pm3, pm3, pm3, cosf, sinf, state0)


def _sortable(x):
    bits = pltpu.bitcast(x + 0.0, I32)
    return bits ^ ((bits >> 31) & 0x7FFFFFFF)


def _kth_largest(count_ge, shape, kf):
    c0 = count_ge(jnp.zeros(shape, I32))
    tau = jnp.where(c0 >= kf, 0, INT_MIN).astype(I32)

    def body(it, tau):
        cand = tau | (jnp.int32(1) << (30 - it))
        return jnp.where(count_ge(cand) >= kf, cand, tau)

    return lax.fori_loop(0, 31, body, tau)


def _tie_cutoff(count_lt, shape, need, nbits):
    def body(it, m):
        cand = m + (jnp.int32(1) << (nbits - 1 - it))
        return jnp.where(count_lt(cand) < need, cand, m)

    return lax.fori_loop(0, nbits, body, jnp.zeros(shape, I32))


def _col_reduce(x, op):
    S, Q = x.shape
    if S > LANES:
        x = op(x.reshape(S // LANES, LANES, Q), axis=0)
    return op(x, axis=0, keepdims=True)


def _dsa_prompt_block(i, S, qa_ref, qi_ref, ptq_ref, o_ref, kb_ref, vt_ref, kib_ref, keys_ref, bias_ref,
                      *, QB, ksel):
    RC = 256 if S % 256 == 0 else LANES
    qiT = qi_ref[0].T
    ptT = ptq_ref[0].T
    zeros64 = jnp.zeros((IDX_DIM, QB), BF16)
    rhs_idx = jnp.concatenate(
        [jnp.concatenate([qiT[h * IDX_DIM:(h + 1) * IDX_DIM].astype(BF16), zeros64], axis=0)
         for h in range(IDX_HEADS)], axis=1)
    w_rows = [ptT[IDX_DIM + h:IDX_DIM + h + 1, :] * (IDX_HEADS ** -0.5) * (IDX_DIM ** -0.5)
              for h in range(IDX_HEADS)]
    t_pos = i * QB + lax.broadcasted_iota(I32, (RC, QB), 1)
    for r in range(S // RC):
        d = jnp.dot(kib_ref[r * RC:(r + 1) * RC, :], rhs_idx, preferred_element_type=F32)
        acc = jnp.maximum(d[:, 0:QB], 0.0) * w_rows[0]
        for h in range(1, IDX_HEADS):
            acc = acc + jnp.maximum(d[:, h * QB:(h + 1) * QB], 0.0) * w_rows[h]
        s_pos = r * RC + lax.broadcasted_iota(I32, (RC, QB), 0)
        keys_ref[r * RC:(r + 1) * RC, :] = _sortable(jnp.where(s_pos <= t_pos, acc, NEG_INF))

    kf = float(ksel)
    NT = S // LANES

    def count(pred):
        acc = jnp.zeros((LANES, QB), F32)
        for r in range(NT):
            acc = jnp.where(pred(keys_ref[r * LANES:(r + 1) * LANES, :], r * LANES), acc + 1.0, acc)
        return jnp.sum(acc, axis=0, keepdims=True)

    tau = _kth_largest(lambda cand: count(lambda k, _: k >= cand), (1, QB), kf)
    row = lax.broadcasted_iota(I32, (LANES, QB), 0)
    t_row = i * QB + lax.broadcasted_iota(I32, (LANES, QB), 1)
    need = kf - count(lambda k, _: k > tau)
    n_eq = count(lambda k, r0: (k == tau) & (r0 + row <= t_row))
    any_split = jnp.max(jnp.where(n_eq > need, 1.0, 0.0)) > 0.0

    def slow():
        return _tie_cutoff(lambda m: count(lambda k, r0: (k == tau) & (r0 + row <= t_row) & (r0 + row < m)),
                           (1, QB), need, int(S).bit_length())

    mcut = lax.cond(any_split, slow, lambda: jnp.full((1, QB), S, I32))
    for r in range(NT):
        k = keys_ref[r * LANES:(r + 1) * LANES, :]
        s_pos = r * LANES + row
        sel = (s_pos <= t_row) & ((k > tau) | ((k == tau) & (s_pos <= mcut)))
        bias_ref[r * LANES:(r + 1) * LANES, :] = jnp.where(sel, 0.0, NEG_INF)

    qaT = qa_ref[0].T * (ATT_HD ** -0.5)
    zq = jnp.zeros((ATT_HD, QB), BF16)
    outs = []
    for j in range(ATT_HEADS // 2):
        r0 = 2 * j * ATT_HD
        top = jnp.concatenate([qaT[r0:r0 + ATT_HD].astype(BF16), zq], axis=1)
        bot = jnp.concatenate([zq, qaT[r0 + ATT_HD:r0 + 2 * ATT_HD].astype(BF16)], axis=1)
        rhs = jnp.concatenate([top, bot], axis=0)
        sT = jnp.dot(kb_ref[0:S, r0:r0 + 2 * ATT_HD], rhs, preferred_element_type=F32)
        for u in range(2):
            h = 2 * j + u
            s = sT[:, u * QB:(u + 1) * QB] + bias_ref[0:S, :]
            m = _col_reduce(s, jnp.max)
            p = jnp.exp(s - m)
            l = _col_reduce(p, jnp.sum)
            oT = jnp.dot(vt_ref[h * ATT_HD:(h + 1) * ATT_HD, 0:S], p.astype(BF16),
                         preferred_element_type=F32)
            outs.append(oT / l)
    o_ref[0] = jnp.concatenate(outs, axis=0).T


def _dsa_prompt_body(qa_ref, k_ref, v_ref, qi_ref, pt_ref, ptq_ref, o_ref,
                     kb_ref, vt_ref, kib_ref, keys_ref, bias_ref, *, T, QB, ksel, NV):
    i = pl.program_id(1)

    @pl.when(i == 0)
    def _():
        kb_ref[...] = k_ref[0].astype(BF16)
        vt_ref[...] = v_ref[0].T.astype(BF16)
        kib_ref[...] = pt_ref[0].astype(BF16)

    per = (T // QB) // NV
    for g in range(NV):
        @pl.when(i // per == g)
        def _(g=g):
            _dsa_prompt_block(i, (g + 1) * per * QB, qa_ref, qi_ref, ptq_ref, o_ref, kb_ref, vt_ref, kib_ref,
                              keys_ref, bias_ref, QB=QB, ksel=ksel)


def _dsa_prompt(pm3, pt3):
    B, T, _ = pm3.shape
    QB = min(LANES, T)
    ksel = min(TOPK_MAX, T // 4)
    W = ATT_HEADS * ATT_HD
    NV = min(4, T // QB)
    return pl.pallas_call(
        functools.partial(_dsa_prompt_body, T=T, QB=QB, ksel=ksel, NV=NV),
        grid=(B, T // QB),
        in_specs=[
            pl.BlockSpec((1, QB, W), lambda b, i: (b, i, 4)),
            pl.BlockSpec((1, T, W), lambda b, i: (b, 0, 5)),
            pl.BlockSpec((1, T, W), lambda b, i: (b, 0, 6)),
            pl.BlockSpec((1, QB, W), lambda b, i: (b, i, 7)),
            pl.BlockSpec((1, T, TAIL_W), lambda b, i: (b, 0, 0)),
            pl.BlockSpec((1, QB, TAIL_W), lambda b, i: (b, i, 0)),
        ],
        out_specs=pl.BlockSpec((1, QB, W), lambda b, i: (b, i, 0)),
        out_shape=jax.ShapeDtypeStruct((B, T, W), F32),
        scratch_shapes=[pltpu.VMEM((T, W), BF16), pltpu.VMEM((W, T), BF16),
                        pltpu.VMEM((T, TAIL_W), BF16), pltpu.VMEM((T, QB), I32),
                        pltpu.VMEM((T, QB), F32)],
        compiler_params=_cparams(("arbitrary", "arbitrary")),
        name="dsa_prompt",
    )(pm3, pm3, pm3, pm3, pt3, pt3)


PAGES_PER_STEP = 16


def _idx_sample_body(ptab_ref, *refs, P):
    del ptab_ref
    page_refs = refs[:P]
    qi_ref, w_ref, kin_ref, sc_ref, scn_ref = refs[P:]
    q = qi_ref[0].astype(BF16)
    w = w_ref[0] * (IDX_HEADS ** -0.5) * (IDX_DIM ** -0.5)
    nq = q.shape[0] // IDX_HEADS

    def scores(keys_t):
        d = jnp.dot(q, keys_t.astype(BF16), preferred_element_type=F32)
        r = jnp.maximum(d, 0.0) * w
        acc = r[0:nq]
        for h in range(1, IDX_HEADS):
            acc = acc + r[h * nq:(h + 1) * nq]
        return acc

    sc_ref[0] = scores(jnp.concatenate([r[0, 0] for r in page_refs], axis=1))

    @pl.when(pl.program_id(1) == 0)
    def _():
        scn_ref[0] = scores(kin_ref[0])


def _idx_sample(page_table, cache_ki_t, layer, qi_flat, w_col, ki_new_t):
    DB, n_pages = page_table.shape
    P = PAGES_PER_STEP
    NCH = n_pages // P
    nq = qi_flat.shape[1] // IDX_HEADS

    def page_spec(p):
        return pl.BlockSpec((1, 1, IDX_DIM, PAGE), lambda b, c, pt: (layer, pt[b, c * P + p], 0, 0))

    grid_spec = pltpu.PrefetchScalarGridSpec(
        num_scalar_prefetch=1,
        grid=(DB, NCH),
        in_specs=[page_spec(p) for p in range(P)] + [
            pl.BlockSpec((1, IDX_HEADS * nq, IDX_DIM), lambda b, c, pt: (b, 0, 0)),
            pl.BlockSpec((1, IDX_HEADS * nq, 1), lambda b, c, pt: (b, 0, 0)),
            pl.BlockSpec((1, IDX_DIM, PAGE), lambda b, c, pt: (b, 0, 0)),
        ],
        out_specs=[
            pl.BlockSpec((1, nq, P * PAGE), lambda b, c, pt: (b, 0, c)),
            pl.BlockSpec((1, nq, PAGE), lambda b, c, pt: (b, 0, 0)),
        ],
    )
    return pl.pallas_call(
        functools.partial(_idx_sample_body, P=P),
        grid_spec=grid_spec,
        out_shape=[jax.ShapeDtypeStruct((DB, nq, NCH * P * PAGE), F32),
                   jax.ShapeDtypeStruct((DB, nq, PAGE), F32)],
        compiler_params=_cparams(("arbitrary", "arbitrary")),
        name="idx_sample",
    )(page_table, *([cache_ki_t] * P), qi_flat, w_col, ki_new_t)


def _select_sample_body(sc_ref, scn_ref, bm_ref, bn_ref, keys_ref, *, GB, nq, past, ksel):
    R = GB * nq
    S = past + PAGE
    NT = S // LANES
    t_of_row = lax.broadcasted_iota(I32, (GB, nq, LANES), 1).reshape(R, LANES)
    lane = lax.broadcasted_iota(I32, (R, LANES), 1)
    new_valid = lane <= t_of_row
    keys_ref[:, :past] = _sortable(sc_ref[...].reshape(R, past))
    keys_ref[:, past:] = _sortable(jnp.where(new_valid, scn_ref[...].reshape(R, PAGE), NEG_INF))
    kf = float(ksel)

    def count(pred):
        acc = jnp.zeros((R, LANES), F32)
        for c in range(NT):
            ok = pred(keys_ref[:, c * LANES:(c + 1) * LANES], c * LANES, new_valid if c == NT - 1 else None)
            acc = jnp.where(ok, acc + 1.0, acc)
        return jnp.sum(acc, axis=1, keepdims=True)

    def tied(k, valid, tau):
        return (k == tau) if valid is None else (k == tau) & valid

    tau = _kth_largest(lambda cand: count(lambda k, c0, v: k >= cand), (R, 1), kf)
    need = kf - count(lambda k, c0, v: k > tau)
    n_eq = count(lambda k, c0, v: tied(k, v, tau))
    any_split = jnp.max(jnp.where(n_eq > need, 1.0, 0.0)) > 0.0

    def slow():
        return _tie_cutoff(lambda m: count(lambda k, c0, v: tied(k, v, tau) & (c0 + lane < m)),
                           (R, 1), need, int(S).bit_length())

    mcut = lax.cond(any_split, slow, lambda: jnp.full((R, 1), S, I32))
    for c in range(NT):
        k = keys_ref[:, c * LANES:(c + 1) * LANES]
        valid = new_valid if c == NT - 1 else None
        sel = (k > tau) | (tied(k, valid, tau) & (c * LANES + lane <= mcut))
        if valid is not None:
            sel = sel & valid
        bias = jnp.where(sel, 0.0, NEG_INF).reshape(GB, nq, LANES)
        if c == NT - 1:
            bn_ref[...] = bias
        else:
            bm_ref[:, :, c * LANES:(c + 1) * LANES] = bias


def _select_sample(sc, scn, ksel):
    DB, nq, past = sc.shape
    GB = int(np.gcd(DB, 8))
    return pl.pallas_call(
        functools.partial(_select_sample_body, GB=GB, nq=nq, past=past, ksel=ksel),
        grid=(DB // GB,),
        in_specs=[pl.BlockSpec((GB, nq, past), lambda b: (b, 0, 0)),
                  pl.BlockSpec((GB, nq, PAGE), lambda b: (b, 0, 0))],
        out_specs=[pl.BlockSpec((GB, nq, past), lambda b: (b, 0, 0)),
                   pl.BlockSpec((GB, nq, PAGE), lambda b: (b, 0, 0))],
        out_shape=[jax.ShapeDtypeStruct((DB, nq, past), F32),
                   jax.ShapeDtypeStruct((DB, nq, PAGE), F32)],
        scratch_shapes=[pltpu.VMEM((GB * nq, past + PAGE), I32)],
        compiler_params=_cparams(("arbitrary",)),
        name="select_sample",
    )(sc, scn)


def _att_sample_body(ptab_ref, *refs, P, NCH, nq):
    del ptab_ref
    k_refs = refs[:P]
    v_refs = refs[P:2 * P]
    q_ref, bm_ref, bn_ref, kn_ref, vn_ref, o_ref, m_ref, l_ref, acc_ref = refs[2 * P:]
    c = pl.program_id(1)

    @pl.when(c == 0)
    def _():
        m_ref[...] = jnp.full(m_ref.shape, NEG_INF, F32)
        l_ref[...] = jnp.zeros(l_ref.shape, F32)
        acc_ref[...] = jnp.zeros(acc_ref.shape, F32)

    q = (q_ref[0] * (ATT_HD ** -0.5)).astype(BF16)

    def update(k_t, v_t, bias):
        s = jnp.dot(q, k_t.astype(BF16), preferred_element_type=F32)
        s = s + jnp.concatenate([bias] * ATT_HEADS, axis=0)
        m_old = m_ref[...]
        m_new = jnp.maximum(m_old, jnp.max(s, axis=1, keepdims=True))
        m_safe = jnp.where(m_new == NEG_INF, 0.0, m_new)
        alpha = jnp.exp(m_old - m_safe)
        p = jnp.exp(s - m_safe)
        l_ref[...] = l_ref[...] * alpha + jnp.sum(p, axis=1, keepdims=True)
        acc_ref[...] = acc_ref[...] * alpha + lax.dot_general(
            p.astype(BF16), v_t.astype(BF16), (((1,), (1,)), ((), ())), preferred_element_type=F32)
        m_ref[...] = m_new

    update(jnp.concatenate([r[0, 0] for r in k_refs], axis=1),
           jnp.concatenate([r[0, 0] for r in v_refs], axis=1), bm_ref[0])

    @pl.when(c == NCH - 1)
    def _():
        update(kn_ref[0], vn_ref[0], bn_ref[0])
        o = acc_ref[...] / l_ref[...]
        head_of_lane = lax.broadcasted_iota(I32, (nq, ATT_HEADS * ATT_HD), 1) // ATT_HD
        out = jnp.zeros((nq, ATT_HEADS * ATT_HD), F32)
        for h in range(ATT_HEADS):
            out = out + jnp.where(head_of_lane == h, o[h * nq:(h + 1) * nq], 0.0)
        o_ref[0] = out


def _att_sample(page_table, cache_k_t, cache_v_t, layer, q_bd, bias_main, bias_new, k_new_t, v_new_t):
    DB, n_pages = page_table.shape
    P = PAGES_PER_STEP
    NCH = n_pages // P
    nq = bias_main.shape[1]
    W = ATT_HEADS * ATT_HD

    def page_spec(p):
        return pl.BlockSpec((1, 1, W, PAGE), lambda b, c, pt: (layer, pt[b, c * P + p], 0, 0))

    grid_spec = pltpu.PrefetchScalarGridSpec(
        num_scalar_prefetch=1,
        grid=(DB, NCH),
        in_specs=[page_spec(p) for p in range(P)] + [page_spec(p) for p in range(P)] + [
            pl.BlockSpec((1, ATT_HEADS * nq, W), lambda b, c, pt: (b, 0, 0)),
            pl.BlockSpec((1, nq, P * PAGE), lambda b, c, pt: (b, 0, c)),
            pl.BlockSpec((1, nq, PAGE), lambda b, c, pt: (b, 0, 0)),
            pl.BlockSpec((1, W, PAGE), lambda b, c, pt: (b, 0, 0)),
            pl.BlockSpec((1, W, PAGE), lambda b, c, pt: (b, 0, 0)),
        ],
        out_specs=pl.BlockSpec((1, nq, W), lambda b, c, pt: (b, 0, 0)),
        scratch_shapes=[pltpu.VMEM((ATT_HEADS * nq, 1), F32), pltpu.VMEM((ATT_HEADS * nq, 1), F32),
                        pltpu.VMEM((ATT_HEADS * nq, W), F32)],
    )
    return pl.pallas_call(
        functools.partial(_att_sample_body, P=P, NCH=NCH, nq=nq),
        grid_spec=grid_spec,
        out_shape=jax.ShapeDtypeStruct((DB, nq, W), F32),
        compiler_params=_cparams(("arbitrary", "arbitrary")),
        name="att_sample",
    )(page_table, *([cache_k_t] * P), *([cache_v_t] * P), q_bd, bias_main, bias_new, k_new_t, v_new_t)


def _dsa_sample(pm3, pt3, cache_k_t, cache_v_t, cache_ki_t, page_table, layer):
    DB, nq, _ = pm3.shape
    W = ATT_HEADS * ATT_HD
    past = page_table.shape[1] * PAGE
    ksel = min(TOPK_MAX, (past + nq) // 4)
    q_a = pm3[:, :, 4 * W:5 * W].reshape(DB, nq, ATT_HEADS, ATT_HD)
    q_i = pm3[:, :, 7 * W:8 * W].reshape(DB, nq, IDX_HEADS, IDX_DIM)
    qi_flat = q_i.transpose(0, 2, 1, 3).reshape(DB, IDX_HEADS * nq, IDX_DIM)
    w_col = pt3[:, :, IDX_DIM:IDX_DIM + IDX_HEADS].transpose(0, 2, 1).reshape(DB, IDX_HEADS * nq, 1)
    q_bd = jnp.einsum("bthd,hg->bhtgd", q_a, jnp.eye(ATT_HEADS, dtype=F32)).reshape(DB, ATT_HEADS * nq, W)

    def new_t(a):
        return jnp.pad(a.transpose(0, 2, 1), ((0, 0), (0, 0), (0, PAGE - nq)))

    sc, scn = _idx_sample(page_table, cache_ki_t, layer, qi_flat, w_col, new_t(pt3[:, :, :IDX_DIM]))
    bias_main, bias_new = _select_sample(sc, scn, ksel)
    return _att_sample(page_table, cache_k_t, cache_v_t, layer, q_bd, bias_main, bias_new,
                       new_t(pm3[:, :, 5 * W:6 * W]), new_t(pm3[:, :, 6 * W:7 * W]))


def _layer_norm(y, g, b):
    mu = jnp.mean(y, axis=-1, keepdims=True)
    var = jnp.mean(jnp.square(y - mu), axis=-1, keepdims=True)
    return (y - mu) * lax.rsqrt(var + LN_EPS) * g + b


def _top2_of4(b0, b1, b2, b3):
    hi01, lo01 = jnp.maximum(b0, b1), jnp.minimum(b0, b1)
    hi23, lo23 = jnp.maximum(b2, b3), jnp.minimum(b2, b3)
    return jnp.maximum(hi01, hi23), jnp.maximum(jnp.minimum(hi01, hi23), jnp.maximum(lo01, lo23))


def _route_rows(aff, biased):
    gscore = []
    for g in range(N_GROUPS):
        m1, m2 = _top2_of4(*biased[g * GROUP_SIZE:(g + 1) * GROUP_SIZE])
        gscore.append(m1 + m2)
    gmax = functools.reduce(jnp.maximum, gscore)
    taken = jnp.zeros_like(gmax, dtype=jnp.bool_)
    gsel = []
    for g in range(N_GROUPS):
        hit = (gscore[g] == gmax) & jnp.logical_not(taken)
        gsel.append(hit)
        taken = taken | hit
    cand, caff = [], []
    for k in range(GROUP_SIZE):
        c = jnp.full_like(gmax, NEG_INF)
        a = jnp.zeros_like(gmax)
        for g in range(N_GROUPS):
            c = jnp.where(gsel[g], biased[g * GROUP_SIZE + k], c)
            a = jnp.where(gsel[g], aff[g * GROUP_SIZE + k], a)
        cand.append(c)
        caff.append(a)
    best = functools.reduce(jnp.maximum, cand)
    taken = jnp.zeros_like(taken)
    first = []
    for k in range(GROUP_SIZE):
        hit = (cand[k] == best) & jnp.logical_not(taken)
        first.append(hit)
        taken = taken | hit
    rest = [jnp.where(first[k], NEG_INF, cand[k]) for k in range(GROUP_SIZE)]
    best2 = functools.reduce(jnp.maximum, rest)
    taken = jnp.zeros_like(taken)
    second = []
    for k in range(GROUP_SIZE):
        hit = (rest[k] == best2) & jnp.logical_not(first[k]) & jnp.logical_not(taken)
        second.append(hit)
        taken = taken | hit
    a1 = functools.reduce(jnp.add, [jnp.where(first[k], caff[k], 0.0) for k in range(GROUP_SIZE)])
    a2 = functools.reduce(jnp.add, [jnp.where(second[k], caff[k], 0.0) for k in range(GROUP_SIZE)])
    tot = a1 + a2
    w1, w2 = a1 / tot, a2 / tot
    gates, chosen = [], []
    for g in range(N_GROUPS):
        for k in range(GROUP_SIZE):
            gates.append(jnp.where(gsel[g] & first[k], w1, jnp.where(gsel[g] & second[k], w2, 0.0)))
            chosen.append(jnp.where(gsel[g] & (first[k] | second[k]), 1.0, 0.0))
    return gates, chosen


def _merge_body(oret_ref, g_ref, oatt_ref, x_ref, gn_ref, wout_ref, lg_ref, lb_ref, rwt_ref, rb_ref,
                x1_ref, gates_ref, wob_ref, *, alpha):
    @pl.when(pl.program_id(0) == 0)
    def _():
        wob_ref[...] = wout_ref[0].astype(BF16)

    parts = []
    for h in range(RET_HEADS):
        sl = slice(h * RET_D, (h + 1) * RET_D)
        o = oret_ref[:, sl]
        mu = jnp.mean(o, axis=-1, keepdims=True)
        var = jnp.mean(jnp.square(o - mu), axis=-1, keepdims=True)
        r = (o - mu) * lax.rsqrt(var + LN_EPS) * gn_ref[0, :, sl]
        g = g_ref[:, sl]
        parts.append(g * jax.nn.sigmoid(g) * r)
    cat = jnp.concatenate(parts + [oatt_ref[...]], axis=1).astype(BF16)
    mix = jnp.dot(cat, wob_ref[...], preferred_element_type=F32)
    x1 = _layer_norm(alpha * x_ref[...] + mix, lg_ref[0], lb_ref[0])
    x1_ref[...] = x1

    logits = lax.dot_general(rwt_ref[...], x1, (((1,), (1,)), ((), ())),
                             precision=lax.Precision.HIGHEST, preferred_element_type=F32)
    aff = jax.nn.sigmoid(logits)
    biased = aff + rb_ref[...]
    gates, chosen = _route_rows([aff[e:e + 1] for e in range(N_EXPERTS)],
                                [biased[e:e + 1] for e in range(N_EXPERTS)])
    tm = x1.shape[0]
    gt = jnp.concatenate(gates + chosen + [jnp.zeros((LANES - 2 * N_EXPERTS, tm), F32)], axis=0)
    gates_ref[...] = gt.T


def _merge(o_ret, pm, o_att, x2d, gn_w, w_out, ln_g, ln_b, rwt, rb_col, layer, tm, alpha):
    n = x2d.shape[0]
    W = RET_HEADS * RET_D
    vec = lambda width: pl.BlockSpec((1, 1, width), lambda i: (layer, 0, 0))
    return pl.pallas_call(
        functools.partial(_merge_body, alpha=alpha),
        grid=(n // tm,),
        in_specs=[
            pl.BlockSpec((tm, W), lambda i: (i, 0)),
            pl.BlockSpec((tm, W), lambda i: (i, 3)),
            pl.BlockSpec((tm, W), lambda i: (i, 0)),
            pl.BlockSpec((tm, D_MODEL), lambda i: (i, 0)),
            vec(W),
            pl.BlockSpec((1, D_MODEL, D_MODEL), lambda i: (layer, 0, 0)),
            vec(D_MODEL), vec(D_MODEL),
            pl.BlockSpec((N_EXPERTS, D_MODEL), lambda i: (0, 0)),
            pl.BlockSpec((N_EXPERTS, 1), lambda i: (0, 0)),
        ],
        out_specs=[pl.BlockSpec((tm, D_MODEL), lambda i: (i, 0)),
                   pl.BlockSpec((tm, LANES), lambda i: (i, 0))],
        out_shape=[jax.ShapeDtypeStruct((n, D_MODEL), F32), jax.ShapeDtypeStruct((n, LANES), F32)],
        scratch_shapes=[pltpu.VMEM((D_MODEL, D_MODEL), BF16)],
        compiler_params=_cparams(("arbitrary",)),
        name="merge",
    )(o_ret, pm, o_att, x2d, gn_w, w_out, ln_g, ln_b, rwt, rb_col)


def _expert(xb, wg, wu, wd):
    a = jnp.dot(xb, wg, preferred_element_type=F32)
    u = jnp.dot(xb, wu, preferred_element_type=F32)
    hid = (a * jax.nn.sigmoid(a) * u).astype(BF16)
    return jnp.dot(hid, wd, preferred_element_type=F32)


def _moe_routed_body(p1_ref, p2_ref, g1_ref, g2_ref, off_ref, cnt_ref,
                     x_ref, wg_ref, wu_ref, wd_ref, lg_ref, lb_ref, out_ref, s_ref, *, TM, R, alpha):
    i = pl.program_id(0)
    e = pl.program_id(1)
    t0 = i * TM

    @pl.when(e == 0)
    def _():
        zeros8 = jnp.zeros((8, D_MODEL), F32)
        for ee in range(N_EXPERTS):
            n = cnt_ref[i * N_EXPERTS + ee]
            s_ref[pl.ds(pl.multiple_of(off_ref[i * N_EXPERTS + ee] + (n // 8) * 8, 8), 8), :] = zeros8
        last = i * N_EXPERTS + N_EXPERTS - 1
        end = off_ref[last] + ((cnt_ref[last] + 7) // 8) * 8
        s_ref[pl.ds(pl.multiple_of(end, 8), R), :] = jnp.zeros((R, D_MODEL), F32)

        def scatter(t, c):
            row = x_ref[pl.ds(t, 1), :]
            s_ref[pl.ds(p1_ref[t0 + t], 1), :] = row
            s_ref[pl.ds(p2_ref[t0 + t], 1), :] = row
            return c

        lax.fori_loop(0, TM, scatter, 0, unroll=8)

    n = cnt_ref[i * N_EXPERTS + e]
    start = off_ref[i * N_EXPERTS + e]
    wg, wu, wd = wg_ref[0, 0], wu_ref[0, 0], wd_ref[0, 0]

    def chunk(c, carry):
        st = pl.multiple_of(start + c * R, 8)
        xs = s_ref[pl.ds(st, R), :]
        y = _expert(xs.astype(BF16), wg, wu, wd)
        mine = c * R + lax.broadcasted_iota(I32, (R, 1), 0) < n
        s_ref[pl.ds(st, R), :] = jnp.where(mine, y, xs)
        return carry

    lax.fori_loop(0, (n + R - 1) // R, chunk, 0)

    @pl.when(e == N_EXPERTS - 1)
    def _():
        def gather(t, c):
            out_ref[pl.ds(t, 1), :] = (g1_ref[t0 + t] * s_ref[pl.ds(p1_ref[t0 + t], 1), :]
                                       + g2_ref[t0 + t] * s_ref[pl.ds(p2_ref[t0 + t], 1), :])
            return c

        lax.fori_loop(0, TM, gather, 0, unroll=8)
        out_ref[...] = _layer_norm(alpha * x_ref[...] + out_ref[...], lg_ref[0], lb_ref[0])


def _moe_routed(x1, gates, w_gate, w_up, w_down, ln_g, ln_b, layer, tm, alpha):
    n = x1.shape[0]
    dff = w_gate.shape[-1]
    nt = n // tm
    R = min(256, tm)
    chosen = gates[:, N_EXPERTS:2 * N_EXPERTS].reshape(nt, tm, N_EXPERTS) > 0.5
    gate_w = gates[:, :N_EXPERTS].reshape(nt, tm, N_EXPERTS)
    csum = jnp.cumsum(chosen.astype(I32), axis=1)
    cnt = csum[:, -1, :]
    cnt8 = (cnt + 7) // 8 * 8
    off = jnp.cumsum(cnt8, axis=1) - cnt8
    pos = off[:, None, :] + csum - 1
    big = 2 ** 30
    p1 = jnp.min(jnp.where(chosen, pos, big), axis=2)
    p2 = jnp.max(jnp.where(chosen, pos, -1), axis=2)
    g1 = jnp.sum(jnp.where(chosen & (pos == p1[:, :, None]), gate_w, 0.0), axis=2)
    g2 = jnp.sum(jnp.where(chosen & (pos == p2[:, :, None]), gate_w, 0.0), axis=2)
    rows = 2 * tm + 8 * N_EXPERTS + R
    vec = pl.BlockSpec((1, 1, D_MODEL), lambda i, e, *_: (layer, 0, 0))
    grid_spec = pltpu.PrefetchScalarGridSpec(
        num_scalar_prefetch=6,
        grid=(nt, N_EXPERTS),
        in_specs=[
            pl.BlockSpec((tm, D_MODEL), lambda i, e, *_: (i, 0), pipeline_mode=pl.Buffered(1)),
            pl.BlockSpec((1, 1, D_MODEL, dff), lambda i, e, *_: (layer, e, 0, 0)),
            pl.BlockSpec((1, 1, D_MODEL, dff), lambda i, e, *_: (layer, e, 0, 0)),
            pl.BlockSpec((1, 1, dff, D_MODEL), lambda i, e, *_: (layer, e, 0, 0)),
            vec, vec,
        ],
        out_specs=pl.BlockSpec((tm, D_MODEL), lambda i, e, *_: (i, 0), pipeline_mode=pl.Buffered(1)),
        scratch_shapes=[pltpu.VMEM((rows, D_MODEL), F32)],
    )
    return pl.pallas_call(
        functools.partial(_moe_routed_body, TM=tm, R=R, alpha=alpha),
        grid_spec=grid_spec,
        out_shape=jax.ShapeDtypeStruct((n, D_MODEL), F32),
        compiler_params=_cparams(("arbitrary", "arbitrary")),
        name="moe_routed",
    )(p1.reshape(-1), p2.reshape(-1), g1.reshape(-1), g2.reshape(-1), off.reshape(-1), cnt.reshape(-1),
      x1, w_gate, w_up, w_down, ln_g, ln_b)


def _moe_body(x1_ref, gates_ref, wg_ref, wu_ref, wd_ref, lg_ref, lb_ref, out_ref, xb_ref, *, alpha):
    e = pl.program_id(1)

    @pl.when(e == 0)
    def _():
        xb_ref[...] = x1_ref[...].astype(BF16)
        out_ref[...] = jnp.zeros(out_ref.shape, F32)

    y = _expert(xb_ref[...], wg_ref[0, 0], wu_ref[0, 0], wd_ref[0, 0])
    gates = gates_ref[...]
    lane = lax.broadcasted_iota(I32, gates.shape, 1)
    g = jnp.sum(jnp.where(lane == e, gates, 0.0), axis=1, keepdims=True)
    out_ref[...] += g * y

    @pl.when(e == N_EXPERTS - 1)
    def _():
        out_ref[...] = _layer_norm(alpha * x1_ref[...] + out_ref[...], lg_ref[0], lb_ref[0])


def _moe(x1, gates, w_gate, w_up, w_down, ln_g, ln_b, layer, tm, alpha):
    n = x1.shape[0]
    dff = w_gate.shape[-1]
    vec = pl.BlockSpec((1, 1, D_MODEL), lambda i, e: (layer, 0, 0))
    return pl.pallas_call(
        functools.partial(_moe_body, alpha=alpha),
        grid=(n // tm, N_EXPERTS),
        in_specs=[
            pl.BlockSpec((tm, D_MODEL), lambda i, e: (i, 0)),
            pl.BlockSpec((tm, LANES), lambda i, e: (i, 0)),
            pl.BlockSpec((1, 1, D_MODEL, dff), lambda i, e: (layer, e, 0, 0)),
            pl.BlockSpec((1, 1, D_MODEL, dff), lambda i, e: (layer, e, 0, 0)),
            pl.BlockSpec((1, 1, dff, D_MODEL), lambda i, e: (layer, e, 0, 0)),
            vec, vec,
        ],
        out_specs=pl.BlockSpec((tm, D_MODEL), lambda i, e: (i, 0)),
        out_shape=jax.ShapeDtypeStruct((n, D_MODEL), F32),
        scratch_shapes=[pltpu.VMEM((tm, D_MODEL), BF16)],
        compiler_params=_cparams(("arbitrary", "arbitrary")),
        name="moe",
    )(x1, gates, w_gate, w_up, w_down, ln_g, ln_b)


def _rope_tables(pos):
    half = RET_D // 2
    inv = ROPE_BASE ** (-jnp.arange(half, dtype=F32) / half)
    ang = pos.astype(F32)[:, None] * inv[None, :]
    cos, sin = jnp.cos(ang), jnp.sin(ang)
    return jnp.concatenate([cos, cos], axis=1), jnp.concatenate([-sin, sin], axis=1)


def kernel(x_prompt, x_sample, cache_k, cache_v, cache_kidx, state_ret, page_table, w_in, ret_gn_w, w_out,
           ln1_g, ln1_b, router_w, router_b, w_gate, w_up, w_down, ln2_g, ln2_b):
    depth = w_in.shape[0]
    B, T, D = x_prompt.shape
    DB, TS, _ = x_sample.shape
    n_pool = cache_k.shape[1]
    W = ATT_HEADS * ATT_HD
    past = page_table.shape[1] * PAGE
    alpha = (2 * depth) ** 0.25

    w_main = w_in[:, :, :MAIN_W].astype(BF16)
    w_tail = jnp.pad(w_in[:, :, MAIN_W:], ((0, 0), (0, 0), (0, TAIL_W - (w_in.shape[2] - MAIN_W)))).astype(BF16)
    cache_k_t = cache_k.transpose(0, 1, 3, 4, 2).reshape(depth, n_pool, W, PAGE)
    cache_v_t = cache_v.transpose(0, 1, 3, 4, 2).reshape(depth, n_pool, W, PAGE)
    cache_ki_t = cache_kidx.transpose(0, 1, 3, 2)
    rwt = router_w.T
    rb_col = router_b.reshape(N_EXPERTS, 1)
    vec3 = lambda a: a.reshape(depth, 1, a.shape[-1])
    gn3, l1g, l1b, l2g, l2b = vec3(ret_gn_w), vec3(ln1_g), vec3(ln1_b), vec3(ln2_g), vec3(ln2_b)
    cos_p, sin_p = _rope_tables(jnp.arange(T))
    cos_s, sin_s = _rope_tables(past + jnp.arange(TS))
    zero_state = jnp.zeros((B, RET_HEADS, RET_D, RET_D), F32)
    C = min(LANES, T)

    wg_b, wu_b, wd_b = w_gate.astype(BF16), w_up.astype(BF16), w_down.astype(BF16)

    def block(x2d, nb, nt, o_att_fn, cosf, sinf, state0, chunk, tm_proj, tm_merge, tm_moe, moe_fn, l):
        pm, pt = _project(x2d, w_main, w_tail, l, tm_proj)
        pm3 = pm.reshape(nb, nt, MAIN_W)
        pt3 = pt.reshape(nb, nt, TAIL_W)
        o_ret, st = _retention(pm3, cosf, sinf, state0, chunk)
        o_att = o_att_fn(pm3, pt3)
        x1, gates = _merge(o_ret.reshape(nb * nt, -1), pm, o_att.reshape(nb * nt, -1), x2d, gn3, w_out,
                           l1g, l1b, rwt, rb_col, l, tm_merge, alpha)
        x2 = moe_fn(x1, gates, wg_b, wu_b, wd_b, l2g, l2b, l, tm_moe, alpha)
        k_a = pm3[:, :, 5 * W:6 * W].reshape(nb, nt, ATT_HEADS, ATT_HD)
        v_a = pm3[:, :, 6 * W:7 * W].reshape(nb, nt, ATT_HEADS, ATT_HD)
        k_i = pt3[:, :, :IDX_DIM]
        return x2, k_a, v_a, k_i, st

    xp = x_prompt.reshape(B * T, D)
    xs = x_sample.reshape(DB * TS, D)
    outs_p, outs_s = [], []
    for l in range(depth):
        xp, *rest = block(xp, B, T, _dsa_prompt, cos_p, sin_p, zero_state, C,
                          min(2048, B * T), min(512, B * T), min(2048, B * T), _moe_routed, l)
        outs_p.append(rest)
        sample_att = functools.partial(_dsa_sample, cache_k_t=cache_k_t, cache_v_t=cache_v_t,
                                       cache_ki_t=cache_ki_t, page_table=page_table, layer=l)
        xs, *rest = block(xs, DB, TS, sample_att, cos_s, sin_s, state_ret[l], TS,
                          DB * TS, DB * TS, DB * TS, _moe, l)
        outs_s.append(rest)
    stack = lambda outs, i: jnp.stack([o[i] for o in outs])
    return (xp.reshape(B, T, D), xs.reshape(DB, TS, D),
            stack(outs_p, 0), stack(outs_p, 1), stack(outs_p, 2), stack(outs_p, 3),
            stack(outs_s, 0), stack(outs_s, 1), stack(outs_s, 2), stack(outs_s, 3))
```

```python
import functools

import numpy as np
import jax
import jax.numpy as jnp
from jax import lax
from jax.experimental import pallas as pl
from jax.experimental.pallas import tpu as pltpu

F32 = jnp.float32
BF16 = jnp.bfloat16
I32 = jnp.int32

RET_HEADS = 4
RET_D = 128
ATT_HEADS = 8
ATT_HD = 64
IDX_HEADS = 8
IDX_DIM = 64
TOPK_MAX = 256
PAGE = 128
N_EXPERTS = 16
N_GROUPS = 4
GROUP_SIZE = N_EXPERTS // N_GROUPS
ROPE_BASE = 10000.0
LN_EPS = 1e-5
D_MODEL = 1024
MAIN_W = 4096
TAIL_W = 128
LANES = 128
VMEM_LIMIT = 56 * 1024 * 1024

INT_MIN = -(2 ** 31)
NEG_INF = float("-inf")


def _log_gammas():
    h = np.arange(RET_HEADS, dtype=np.float32)
    return [float(v) for v in np.log1p(-np.exp2(-5.0 - h)).astype(np.float32)]


def _cparams(sem):
    return pltpu.CompilerParams(dimension_semantics=sem, vmem_limit_bytes=VMEM_LIMIT)


def _proj_body(x_ref, wm_ref, wt_ref, pm_ref, pt_ref, xb_ref):
    @pl.when(pl.program_id(1) == 0)
    def _():
        xb = x_ref[...].astype(BF16)
        xb_ref[...] = xb
        pt_ref[...] = jnp.dot(xb, wt_ref[0], preferred_element_type=F32)

    pm_ref[...] = jnp.dot(xb_ref[...], wm_ref[0], preferred_element_type=F32)


def _project(x2d, w_main, w_tail, layer, tm):
    n = x2d.shape[0]
    tn = 512
    return pl.pallas_call(
        _proj_body,
        grid=(n // tm, MAIN_W // tn),
        in_specs=[
            pl.BlockSpec((tm, D_MODEL), lambda i, j: (i, 0)),
            pl.BlockSpec((1, D_MODEL, tn), lambda i, j: (layer, 0, j)),
            pl.BlockSpec((1, D_MODEL, TAIL_W), lambda i, j: (layer, 0, 0)),
        ],
        out_specs=[
            pl.BlockSpec((tm, tn), lambda i, j: (i, j)),
            pl.BlockSpec((tm, TAIL_W), lambda i, j: (i, 0)),
        ],
        out_shape=[jax.ShapeDtypeStruct((n, MAIN_W), F32), jax.ShapeDtypeStruct((n, TAIL_W), F32)],
        scratch_shapes=[pltpu.VMEM((tm, D_MODEL), BF16)],
        compiler_params=_cparams(("arbitrary", "arbitrary")),
        name="proj",
    )(x2d, w_main, w_tail)


def _ret_body(q_ref, k_ref, v_ref, cos_ref, sin_ref, s0_ref, o_ref, sout_ref, st_ref, *, C, NC):
    c = pl.program_id(1)
    CP = max(C, LANES)

    @pl.when(c == 0)
    def _():
        st_ref[...] = s0_ref[0]

    def pad(a):
        if CP == C:
            return a
        return jnp.concatenate([a, jnp.zeros((CP - C, a.shape[1]), a.dtype)], axis=0)

    cosf = pad(cos_ref[...])
    sinf = pad(sin_ref[...])
    ii = lax.broadcasted_iota(I32, (CP, CP), 0)
    jj = lax.broadcasted_iota(I32, (CP, CP), 1)
    diff = (ii - jj).astype(F32)
    pos = lax.broadcasted_iota(I32, (CP, 1), 0).astype(F32)
    for h, lg in enumerate(_log_gammas()):
        sl = slice(h * RET_D, (h + 1) * RET_D)
        q = pad(q_ref[0, :, sl])
        k = pad(k_ref[0, :, sl])
        vb = pad(v_ref[0, :, sl]).astype(BF16)
        q = q * cosf + pltpu.roll(q, RET_D // 2, 1) * sinf
        k = (k * cosf + pltpu.roll(k, RET_D // 2, 1) * sinf) * (RET_D ** -0.5)
        decay = jnp.where(diff >= 0, jnp.exp(lg * jnp.maximum(diff, 0.0)), 0.0)
        qb = q.astype(BF16)
        scores = lax.dot_general(qb, k.astype(BF16), (((1,), (1,)), ((), ())),
                                 preferred_element_type=F32) * decay
        inner = jnp.dot(scores.astype(BF16), vb, preferred_element_type=F32)
        st = st_ref[h]
        cross = jnp.dot(qb, st.astype(BF16), preferred_element_type=F32) * jnp.exp(lg * (pos + 1.0))
        o_ref[0, :, sl] = (inner + cross)[:C]
        kd = k * jnp.exp(lg * (C - 1.0 - pos))
        st_ref[h] = st * float(np.exp(np.float32(lg) * np.float32(C))) + jnp.dot(
            kd.T.astype(BF16), vb, preferred_element_type=F32)

    @pl.when(c == NC - 1)
    def _():
        sout_ref[0] = st_ref[...]


def _retention(pm3, cosf, sinf, state0, C):
    B, T, _ = pm3.shape
    NC = T // C
    W = RET_HEADS * RET_D
    return pl.pallas_call(
        functools.partial(_ret_body, C=C, NC=NC),
        grid=(B, NC),
        in_specs=[
            pl.BlockSpec((1, C, W), lambda b, c: (b, c, 0)),
            pl.BlockSpec((1, C, W), lambda b, c: (b, c, 1)),
            pl.BlockSpec((1, C, W), lambda b, c: (b, c, 2)),
            pl.BlockSpec((C, RET_D), lambda b, c: (c, 0)),
            pl.BlockSpec((C, RET_D), lambda b, c: (c, 0)),
            pl.BlockSpec((1, RET_HEADS, RET_D, RET_D), lambda b, c: (b, 0, 0, 0)),
        ],
        out_specs=[
            pl.BlockSpec((1, C, W), lambda b, c: (b, c, 0)),
            pl.BlockSpec((1, RET_HEADS, RET_D, RET_D), lambda b, c: (b, 0, 0, 0)),
        ],
        out_shape=[jax.ShapeDtypeStruct((B, T, W), F32),
                   jax.ShapeDtypeStruct((B, RET_HEADS, RET_D, RET_D), F32)],
        scratch_shapes=[pltpu.VMEM((RET_HEADS, RET_D, RET_D), F32)],
        compiler_params=_cparams(("arbitrary", "arbitrary")),
        name="retention",
    )(pm3, pm3, pm3, cosf, sinf, state0)


def _sortable(x):
    bits = pltpu.bitcast(x + 0.0, I32)
    return bits ^ ((bits >> 31) & 0x7FFFFFFF)


def _kth_largest(count_ge, shape, kf):
    c0 = count_ge(jnp.zeros(shape, I32))
    tau = jnp.where(c0 >= kf, 0, INT_MIN).astype(I32)

    def body(it, tau):
        cand = tau | (jnp.int32(1) << (30 - it))
        return jnp.where(count_ge(cand) >= kf, cand, tau)

    return lax.fori_loop(0, 31, body, tau)


def _tie_cutoff(count_lt, shape, need, nbits):
    def body(it, m):
        cand = m + (jnp.int32(1) << (nbits - 1 - it))
        return jnp.where(count_lt(cand) < need, cand, m)

    return lax.fori_loop(0, nbits, body, jnp.zeros(shape, I32))


def _col_reduce(x, op):
    S, Q = x.shape
    if S > LANES:
        x = op(x.reshape(S // LANES, LANES, Q), axis=0)
    return op(x, axis=0, keepdims=True)


def _dsa_prompt_block(i, S, qa_ref, qi_ref, ptq_ref, o_ref, kb_ref, vt_ref, kib_ref, keys_ref, bias_ref,
                      *, QB, ksel):
    RC = 256 if S % 256 == 0 else LANES
    qiT = qi_ref[0].T
    ptT = ptq_ref[0].T
    zeros64 = jnp.zeros((IDX_DIM, QB), BF16)
    rhs_idx = jnp.concatenate(
        [jnp.concatenate([qiT[h * IDX_DIM:(h + 1) * IDX_DIM].astype(BF16), zeros64], axis=0)
         for h in range(IDX_HEADS)], axis=1)
    w_rows = [ptT[IDX_DIM + h:IDX_DIM + h + 1, :] * (IDX_HEADS ** -0.5) * (IDX_DIM ** -0.5)
              for h in range(IDX_HEADS)]
    t_pos = i * QB + lax.broadcasted_iota(I32, (RC, QB), 1)
    for r in range(S // RC):
        d = jnp.dot(kib_ref[r * RC:(r + 1) * RC, :], rhs_idx, preferred_element_type=F32)
        acc = jnp.maximum(d[:, 0:QB], 0.0) * w_rows[0]
        for h in range(1, IDX_HEADS):
            acc = acc + jnp.maximum(d[:, h * QB:(h + 1) * QB], 0.0) * w_rows[h]
        s_pos = r * RC + lax.broadcasted_iota(I32, (RC, QB), 0)
        keys_ref[r * RC:(r + 1) * RC, :] = _sortable(jnp.where(s_pos <= t_pos, acc, NEG_INF))

    kf = float(ksel)
    NT = S // LANES

    def count(pred):
        acc = jnp.zeros((LANES, QB), F32)
        for r in range(NT):
            acc = jnp.where(pred(keys_ref[r * LANES:(r + 1) * LANES, :], r * LANES), acc + 1.0, acc)
        return jnp.sum(acc, axis=0, keepdims=True)

    tau = _kth_largest(lambda cand: count(lambda k, _: k >= cand), (1, QB), kf)
    row = lax.broadcasted_iota(I32, (LANES, QB), 0)
    t_row = i * QB + lax.broadcasted_iota(I32, (LANES, QB), 1)
    need = kf - count(lambda k, _: k > tau)
    n_eq = count(lambda k, r0: (k == tau) & (r0 + row <= t_row))
    any_split = jnp.max(jnp.where(n_eq > need, 1.0, 0.0)) > 0.0

    def slow():
        return _tie_cutoff(lambda m: count(lambda k, r0: (k == tau) & (r0 + row <= t_row) & (r0 + row < m)),
                           (1, QB), need, int(S).bit_length())

    mcut = lax.cond(any_split, slow, lambda: jnp.full((1, QB), S, I32))
    for r in range(NT):
        k = keys_ref[r * LANES:(r + 1) * LANES, :]
        s_pos = r * LANES + row
        sel = (s_pos <= t_row) & ((k > tau) | ((k == tau) & (s_pos <= mcut)))
        bias_ref[r * LANES:(r + 1) * LANES, :] = jnp.where(sel, 0.0, NEG_INF)

    qaT = qa_ref[0].T * (ATT_HD ** -0.5)
    zq = jnp.zeros((ATT_HD, QB), BF16)
    outs = []
    for j in range(ATT_HEADS // 2):
        r0 = 2 * j * ATT_HD
        top = jnp.concatenate([qaT[r0:r0 + ATT_HD].astype(BF16), zq], axis=1)
        bot = jnp.concatenate([zq, qaT[r0 + ATT_HD:r0 + 2 * ATT_HD].astype(BF16)], axis=1)
        rhs = jnp.concatenate([top, bot], axis=0)
        sT = jnp.dot(kb_ref[0:S, r0:r0 + 2 * ATT_HD], rhs, preferred_element_type=F32)
        for u in range(2):
            h = 2 * j + u
            s = sT[:, u * QB:(u + 1) * QB] + bias_ref[0:S, :]
            m = _col_reduce(s, jnp.max)
            p = jnp.exp(s - m)
            l = _col_reduce(p, jnp.sum)
            oT = jnp.dot(vt_ref[h * ATT_HD:(h + 1) * ATT_HD, 0:S], p.astype(BF16),
                         preferred_element_type=F32)
            outs.append(oT / l)
    o_ref[0] = jnp.concatenate(outs, axis=0).T


def _dsa_prompt_body(qa_ref, k_ref, v_ref, qi_ref, pt_ref, ptq_ref, o_ref,
                     kb_ref, vt_ref, kib_ref, keys_ref, bias_ref, *, T, QB, ksel, NV):
    i = pl.program_id(1)

    @pl.when(i == 0)
    def _():
        kb_ref[...] = k_ref[0].astype(BF16)
        vt_ref[...] = v_ref[0].T.astype(BF16)
        kib_ref[...] = pt_ref[0].astype(BF16)

    per = (T // QB) // NV
    for g in range(NV):
        @pl.when(i // per == g)
        def _(g=g):
            _dsa_prompt_block(i, (g + 1) * per * QB, qa_ref, qi_ref, ptq_ref, o_ref, kb_ref, vt_ref, kib_ref,
                              keys_ref, bias_ref, QB=QB, ksel=ksel)


def _dsa_prompt(pm3, pt3):
    B, T, _ = pm3.shape
    QB = min(LANES, T)
    ksel = min(TOPK_MAX, T // 4)
    W = ATT_HEADS * ATT_HD
    NV = min(8, T // QB)
    return pl.pallas_call(
        functools.partial(_dsa_prompt_body, T=T, QB=QB, ksel=ksel, NV=NV),
        grid=(B, T // QB),
        in_specs=[
            pl.BlockSpec((1, QB, W), lambda b, i: (b, i, 4)),
            pl.BlockSpec((1, T, W), lambda b, i: (b, 0, 5)),
            pl.BlockSpec((1, T, W), lambda b, i: (b, 0, 6)),
            pl.BlockSpec((1, QB, W), lambda b, i: (b, i, 7)),
            pl.BlockSpec((1, T, TAIL_W), lambda b, i: (b, 0, 0)),
            pl.BlockSpec((1, QB, TAIL_W), lambda b, i: (b, i, 0)),
        ],
        out_specs=pl.BlockSpec((1, QB, W), lambda b, i: (b, i, 0)),
        out_shape=jax.ShapeDtypeStruct((B, T, W), F32),
        scratch_shapes=[pltpu.VMEM((T, W), BF16), pltpu.VMEM((W, T), BF16),
                        pltpu.VMEM((T, TAIL_W), BF16), pltpu.VMEM((T, QB), I32),
                        pltpu.VMEM((T, QB), F32)],
        compiler_params=_cparams(("arbitrary", "arbitrary")),
        name="dsa_prompt",
    )(pm3, pm3, pm3, pm3, pt3, pt3)


PAGES_PER_STEP = 16


def _idx_sample_body(ptab_ref, *refs, P):
    del ptab_ref
    page_refs = refs[:P]
    qi_ref, w_ref, kin_ref, sc_ref, scn_ref = refs[P:]
    q = qi_ref[0].astype(BF16)
    w = w_ref[0] * (IDX_HEADS ** -0.5) * (IDX_DIM ** -0.5)
    nq = q.shape[0] // IDX_HEADS

    def scores(keys_t):
        d = jnp.dot(q, keys_t.astype(BF16), preferred_element_type=F32)
        r = jnp.maximum(d, 0.0) * w
        acc = r[0:nq]
        for h in range(1, IDX_HEADS):
            acc = acc + r[h * nq:(h + 1) * nq]
        return acc

    sc_ref[0] = scores(jnp.concatenate([r[0, 0] for r in page_refs], axis=1))

    @pl.when(pl.program_id(1) == 0)
    def _():
        scn_ref[0] = scores(kin_ref[0])


def _idx_sample(page_table, cache_ki_t, layer, qi_flat, w_col, ki_new_t):
    DB, n_pages = page_table.shape
    P = PAGES_PER_STEP
    NCH = n_pages // P
    nq = qi_flat.shape[1] // IDX_HEADS

    def page_spec(p):
        return pl.BlockSpec((1, 1, IDX_DIM, PAGE), lambda b, c, pt: (layer, pt[b, c * P + p], 0, 0))

    grid_spec = pltpu.PrefetchScalarGridSpec(
        num_scalar_prefetch=1,
        grid=(DB, NCH),
        in_specs=[page_spec(p) for p in range(P)] + [
            pl.BlockSpec((1, IDX_HEADS * nq, IDX_DIM), lambda b, c, pt: (b, 0, 0)),
            pl.BlockSpec((1, IDX_HEADS * nq, 1), lambda b, c, pt: (b, 0, 0)),
            pl.BlockSpec((1, IDX_DIM, PAGE), lambda b, c, pt: (b, 0, 0)),
        ],
        out_specs=[
            pl.BlockSpec((1, nq, P * PAGE), lambda b, c, pt: (b, 0, c)),
            pl.BlockSpec((1, nq, PAGE), lambda b, c, pt: (b, 0, 0)),
        ],
    )
    return pl.pallas_call(
        functools.partial(_idx_sample_body, P=P),
        grid_spec=grid_spec,
        out_shape=[jax.ShapeDtypeStruct((DB, nq, NCH * P * PAGE), F32),
                   jax.ShapeDtypeStruct((DB, nq, PAGE), F32)],
        compiler_params=_cparams(("arbitrary", "arbitrary")),
        name="idx_sample",
    )(page_table, *([cache_ki_t] * P), qi_flat, w_col, ki_new_t)


def _select_sample_body(sc_ref, scn_ref, bm_ref, bn_ref, keys_ref, *, GB, nq, past, ksel):
    R = GB * nq
    S = past + PAGE
    NT = S // LANES
    t_of_row = lax.broadcasted_iota(I32, (GB, nq, LANES), 1).reshape(R, LANES)
    lane = lax.broadcasted_iota(I32, (R, LANES), 1)
    new_valid = lane <= t_of_row
    keys_ref[:, :past] = _sortable(sc_ref[...].reshape(R, past))
    keys_ref[:, past:] = _sortable(jnp.where(new_valid, scn_ref[...].reshape(R, PAGE), NEG_INF))
    kf = float(ksel)

    def count(pred):
        acc = jnp.zeros((R, LANES), F32)
        for c in range(NT):
            ok = pred(keys_ref[:, c * LANES:(c + 1) * LANES], c * LANES, new_valid if c == NT - 1 else None)
            acc = jnp.where(ok, acc + 1.0, acc)
        return jnp.sum(acc, axis=1, keepdims=True)

    def tied(k, valid, tau):
        return (k == tau) if valid is None else (k == tau) & valid

    tau = _kth_largest(lambda cand: count(lambda k, c0, v: k >= cand), (R, 1), kf)
    need = kf - count(lambda k, c0, v: k > tau)
    n_eq = count(lambda k, c0, v: tied(k, v, tau))
    any_split = jnp.max(jnp.where(n_eq > need, 1.0, 0.0)) > 0.0

    def slow():
        return _tie_cutoff(lambda m: count(lambda k, c0, v: tied(k, v, tau) & (c0 + lane < m)),
                           (R, 1), need, int(S).bit_length())

    mcut = lax.cond(any_split, slow, lambda: jnp.full((R, 1), S, I32))
    for c in range(NT):
        k = keys_ref[:, c * LANES:(c + 1) * LANES]
        valid = new_valid if c == NT - 1 else None
        sel = (k > tau) | (tied(k, valid, tau) & (c * LANES + lane <= mcut))
        if valid is not None:
            sel = sel & valid
        bias = jnp.where(sel, 0.0, NEG_INF).reshape(GB, nq, LANES)
        if c == NT - 1:
            bn_ref[...] = bias
        else:
            bm_ref[:, :, c * LANES:(c + 1) * LANES] = bias


def _select_sample(sc, scn, ksel):
    DB, nq, past = sc.shape
    GB = int(np.gcd(DB, 8))
    return pl.pallas_call(
        functools.partial(_select_sample_body, GB=GB, nq=nq, past=past, ksel=ksel),
        grid=(DB // GB,),
        in_specs=[pl.BlockSpec((GB, nq, past), lambda b: (b, 0, 0)),
                  pl.BlockSpec((GB, nq, PAGE), lambda b: (b, 0, 0))],
        out_specs=[pl.BlockSpec((GB, nq, past), lambda b: (b, 0, 0)),
                   pl.BlockSpec((GB, nq, PAGE), lambda b: (b, 0, 0))],
        out_shape=[jax.ShapeDtypeStruct((DB, nq, past), F32),
                   jax.ShapeDtypeStruct((DB, nq, PAGE), F32)],
        scratch_shapes=[pltpu.VMEM((GB * nq, past + PAGE), I32)],
        compiler_params=_cparams(("arbitrary",)),
        name="select_sample",
    )(sc, scn)


def _att_sample_body(ptab_ref, *refs, P, NCH, nq):
    del ptab_ref
    k_refs = refs[:P]
    v_refs = refs[P:2 * P]
    q_ref, bm_ref, bn_ref, kn_ref, vn_ref, o_ref, m_ref, l_ref, acc_ref = refs[2 * P:]
    c = pl.program_id(1)

    @pl.when(c == 0)
    def _():
        m_ref[...] = jnp.full(m_ref.shape, NEG_INF, F32)
        l_ref[...] = jnp.zeros(l_ref.shape, F32)
        acc_ref[...] = jnp.zeros(acc_ref.shape, F32)

    q = (q_ref[0] * (ATT_HD ** -0.5)).astype(BF16)

    def update(k_t, v_t, bias):
        s = jnp.dot(q, k_t.astype(BF16), preferred_element_type=F32)
        s = s + jnp.concatenate([bias] * ATT_HEADS, axis=0)
        m_old = m_ref[...]
        m_new = jnp.maximum(m_old, jnp.max(s, axis=1, keepdims=True))
        m_safe = jnp.where(m_new == NEG_INF, 0.0, m_new)
        alpha = jnp.exp(m_old - m_safe)
        p = jnp.exp(s - m_safe)
        l_ref[...] = l_ref[...] * alpha + jnp.sum(p, axis=1, keepdims=True)
        acc_ref[...] = acc_ref[...] * alpha + lax.dot_general(
            p.astype(BF16), v_t.astype(BF16), (((1,), (1,)), ((), ())), preferred_element_type=F32)
        m_ref[...] = m_new

    update(jnp.concatenate([r[0, 0] for r in k_refs], axis=1),
           jnp.concatenate([r[0, 0] for r in v_refs], axis=1), bm_ref[0])

    @pl.when(c == NCH - 1)
    def _():
        update(kn_ref[0], vn_ref[0], bn_ref[0])
        o = acc_ref[...] / l_ref[...]
        head_of_lane = lax.broadcasted_iota(I32, (nq, ATT_HEADS * ATT_HD), 1) // ATT_HD
        out = jnp.zeros((nq, ATT_HEADS * ATT_HD), F32)
        for h in range(ATT_HEADS):
            out = out + jnp.where(head_of_lane == h, o[h * nq:(h + 1) * nq], 0.0)
        o_ref[0] = out


def _att_sample(page_table, cache_k_t, cache_v_t, layer, q_bd, bias_main, bias_new, k_new_t, v_new_t):
    DB, n_pages = page_table.shape
    P = PAGES_PER_STEP
    NCH = n_pages // P
    nq = bias_main.shape[1]
    W = ATT_HEADS * ATT_HD

    def page_spec(p):
        return pl.BlockSpec((1, 1, W, PAGE), lambda b, c, pt: (layer, pt[b, c * P + p], 0, 0))

    grid_spec = pltpu.PrefetchScalarGridSpec(
        num_scalar_prefetch=1,
        grid=(DB, NCH),
        in_specs=[page_spec(p) for p in range(P)] + [page_spec(p) for p in range(P)] + [
            pl.BlockSpec((1, ATT_HEADS * nq, W), lambda b, c, pt: (b, 0, 0)),
            pl.BlockSpec((1, nq, P * PAGE), lambda b, c, pt: (b, 0, c)),
            pl.BlockSpec((1, nq, PAGE), lambda b, c, pt: (b, 0, 0)),
            pl.BlockSpec((1, W, PAGE), lambda b, c, pt: (b, 0, 0)),
            pl.BlockSpec((1, W, PAGE), lambda b, c, pt: (b, 0, 0)),
        ],
        out_specs=pl.BlockSpec((1, nq, W), lambda b, c, pt: (b, 0, 0)),
        scratch_shapes=[pltpu.VMEM((ATT_HEADS * nq, 1), F32), pltpu.VMEM((ATT_HEADS * nq, 1), F32),
                        pltpu.VMEM((ATT_HEADS * nq, W), F32)],
    )
    return pl.pallas_call(
        functools.partial(_att_sample_body, P=P, NCH=NCH, nq=nq),
        grid_spec=grid_spec,
        out_shape=jax.ShapeDtypeStruct((DB, nq, W), F32),
        compiler_params=_cparams(("arbitrary", "arbitrary")),
        name="att_sample",
    )(page_table, *([cache_k_t] * P), *([cache_v_t] * P), q_bd, bias_main, bias_new, k_new_t, v_new_t)


def _dsa_sample(pm3, pt3, cache_k_t, cache_v_t, cache_ki_t, page_table, layer):
    DB, nq, _ = pm3.shape
    W = ATT_HEADS * ATT_HD
    past = page_table.shape[1] * PAGE
    ksel = min(TOPK_MAX, (past + nq) // 4)
    q_a = pm3[:, :, 4 * W:5 * W].reshape(DB, nq, ATT_HEADS, ATT_HD)
    q_i = pm3[:, :, 7 * W:8 * W].reshape(DB, nq, IDX_HEADS, IDX_DIM)
    qi_flat = q_i.transpose(0, 2, 1, 3).reshape(DB, IDX_HEADS * nq, IDX_DIM)
    w_col = pt3[:, :, IDX_DIM:IDX_DIM + IDX_HEADS].transpose(0, 2, 1).reshape(DB, IDX_HEADS * nq, 1)
    q_bd = jnp.einsum("bthd,hg->bhtgd", q_a, jnp.eye(ATT_HEADS, dtype=F32)).reshape(DB, ATT_HEADS * nq, W)

    def new_t(a):
        return jnp.pad(a.transpose(0, 2, 1), ((0, 0), (0, 0), (0, PAGE - nq)))

    sc, scn = _idx_sample(page_table, cache_ki_t, layer, qi_flat, w_col, new_t(pt3[:, :, :IDX_DIM]))
    bias_main, bias_new = _select_sample(sc, scn, ksel)
    return _att_sample(page_table, cache_k_t, cache_v_t, layer, q_bd, bias_main, bias_new,
                       new_t(pm3[:, :, 5 * W:6 * W]), new_t(pm3[:, :, 6 * W:7 * W]))


def _layer_norm(y, g, b):
    mu = jnp.mean(y, axis=-1, keepdims=True)
    var = jnp.mean(jnp.square(y - mu), axis=-1, keepdims=True)
    return (y - mu) * lax.rsqrt(var + LN_EPS) * g + b


def _top2_of4(b0, b1, b2, b3):
    hi01, lo01 = jnp.maximum(b0, b1), jnp.minimum(b0, b1)
    hi23, lo23 = jnp.maximum(b2, b3), jnp.minimum(b2, b3)
    return jnp.maximum(hi01, hi23), jnp.maximum(jnp.minimum(hi01, hi23), jnp.maximum(lo01, lo23))


def _route_rows(aff, biased):
    gscore = []
    for g in range(N_GROUPS):
        m1, m2 = _top2_of4(*biased[g * GROUP_SIZE:(g + 1) * GROUP_SIZE])
        gscore.append(m1 + m2)
    gmax = functools.reduce(jnp.maximum, gscore)
    taken = jnp.zeros_like(gmax, dtype=jnp.bool_)
    gsel = []
    for g in range(N_GROUPS):
        hit = (gscore[g] == gmax) & jnp.logical_not(taken)
        gsel.append(hit)
        taken = taken | hit
    cand, caff = [], []
    for k in range(GROUP_SIZE):
        c = jnp.full_like(gmax, NEG_INF)
        a = jnp.zeros_like(gmax)
        for g in range(N_GROUPS):
            c = jnp.where(gsel[g], biased[g * GROUP_SIZE + k], c)
            a = jnp.where(gsel[g], aff[g * GROUP_SIZE + k], a)
        cand.append(c)
        caff.append(a)
    best = functools.reduce(jnp.maximum, cand)
    taken = jnp.zeros_like(taken)
    first = []
    for k in range(GROUP_SIZE):
        hit = (cand[k] == best) & jnp.logical_not(taken)
        first.append(hit)
        taken = taken | hit
    rest = [jnp.where(first[k], NEG_INF, cand[k]) for k in range(GROUP_SIZE)]
    best2 = functools.reduce(jnp.maximum, rest)
    taken = jnp.zeros_like(taken)
    second = []
    for k in range(GROUP_SIZE):
        hit = (rest[k] == best2) & jnp.logical_not(first[k]) & jnp.logical_not(taken)
        second.append(hit)
        taken = taken | hit
    a1 = functools.reduce(jnp.add, [jnp.where(first[k], caff[k], 0.0) for k in range(GROUP_SIZE)])
    a2 = functools.reduce(jnp.add, [jnp.where(second[k], caff[k], 0.0) for k in range(GROUP_SIZE)])
    tot = a1 + a2
    w1, w2 = a1 / tot, a2 / tot
    gates, chosen = [], []
    for g in range(N_GROUPS):
        for k in range(GROUP_SIZE):
            gates.append(jnp.where(gsel[g] & first[k], w1, jnp.where(gsel[g] & second[k], w2, 0.0)))
            chosen.append(jnp.where(gsel[g] & (first[k] | second[k]), 1.0, 0.0))
    return gates, chosen


def _merge_body(oret_ref, g_ref, oatt_ref, x_ref, gn_ref, wout_ref, lg_ref, lb_ref, rwt_ref, rb_ref,
                x1_ref, gates_ref, route_ref, cnt_ref, wob_ref, carry_ref, *, alpha, group):
    @pl.when(pl.program_id(0) == 0)
    def _():
        wob_ref[...] = wout_ref[0].astype(BF16)

    @pl.when(pl.program_id(0) % group == 0)
    def _():
        carry_ref[...] = jnp.zeros(carry_ref.shape, F32)

    parts = []
    for h in range(RET_HEADS):
        sl = slice(h * RET_D, (h + 1) * RET_D)
        o = oret_ref[:, sl]
        mu = jnp.mean(o, axis=-1, keepdims=True)
        var = jnp.mean(jnp.square(o - mu), axis=-1, keepdims=True)
        r = (o - mu) * lax.rsqrt(var + LN_EPS) * gn_ref[0, :, sl]
        g = g_ref[:, sl]
        parts.append(g * jax.nn.sigmoid(g) * r)
    cat = jnp.concatenate(parts + [oatt_ref[...]], axis=1).astype(BF16)
    mix = jnp.dot(cat, wob_ref[...], preferred_element_type=F32)
    x1 = _layer_norm(alpha * x_ref[...] + mix, lg_ref[0], lb_ref[0])
    x1_ref[...] = x1

    logits = lax.dot_general(rwt_ref[...], x1, (((1,), (1,)), ((), ())),
                             precision=lax.Precision.HIGHEST, preferred_element_type=F32)
    aff = jax.nn.sigmoid(logits)
    biased = aff + rb_ref[...]
    gates, chosen = _route_rows([aff[e:e + 1] for e in range(N_EXPERTS)],
                                [biased[e:e + 1] for e in range(N_EXPERTS)])
    tm = x1.shape[0]
    gt = jnp.concatenate(gates + chosen + [jnp.zeros((LANES - 2 * N_EXPERTS, tm), F32)], axis=0)
    gates_ref[...] = gt.T

    cmat = jnp.concatenate(chosen, axis=0)
    upper = (lax.broadcasted_iota(I32, (tm, tm), 0) <= lax.broadcasted_iota(I32, (tm, tm), 1))
    csum = jnp.dot(cmat.astype(BF16), jnp.where(upper, 1.0, 0.0).astype(BF16),
                   preferred_element_type=F32)
    carry = carry_ref[...]
    rank = carry[:, 0:1] + csum - 1.0
    carry = carry + csum[:, tm - 1:tm]
    carry_ref[...] = carry
    cnt_ref[0] = carry
    zero = jnp.zeros((1, tm), F32)
    seen, e1, e2, r1, r2, g1, g2 = zero, zero, zero, zero, zero, zero, zero
    for e in range(N_EXPERTS):
        hit = chosen[e] > 0.5
        first = hit & (seen == 0.0)
        second = hit & (seen == 1.0)
        e1 = jnp.where(first, float(e), e1)
        e2 = jnp.where(second, float(e), e2)
        r1 = jnp.where(first, rank[e:e + 1], r1)
        r2 = jnp.where(second, rank[e:e + 1], r2)
        g1 = jnp.where(first, gates[e], g1)
        g2 = jnp.where(second, gates[e], g2)
        seen = seen + chosen[e]
    route_ref[...] = jnp.concatenate([e1, e2, r1, r2, g1, g2, zero, zero], axis=0)


def _merge(o_ret, pm, o_att, x2d, gn_w, w_out, ln_g, ln_b, rwt, rb_col, layer, tm, alpha, group):
    n = x2d.shape[0]
    W = RET_HEADS * RET_D
    vec = lambda width: pl.BlockSpec((1, 1, width), lambda i: (layer, 0, 0))
    return pl.pallas_call(
        functools.partial(_merge_body, alpha=alpha, group=group),
        grid=(n // tm,),
        in_specs=[
            pl.BlockSpec((tm, W), lambda i: (i, 0)),
            pl.BlockSpec((tm, W), lambda i: (i, 3)),
            pl.BlockSpec((tm, W), lambda i: (i, 0)),
            pl.BlockSpec((tm, D_MODEL), lambda i: (i, 0)),
            vec(W),
            pl.BlockSpec((1, D_MODEL, D_MODEL), lambda i: (layer, 0, 0)),
            vec(D_MODEL), vec(D_MODEL),
            pl.BlockSpec((N_EXPERTS, D_MODEL), lambda i: (0, 0)),
            pl.BlockSpec((N_EXPERTS, 1), lambda i: (0, 0)),
        ],
        out_specs=[pl.BlockSpec((tm, D_MODEL), lambda i: (i, 0)),
                   pl.BlockSpec((tm, LANES), lambda i: (i, 0)),
                   pl.BlockSpec((8, tm), lambda i: (0, i)),
                   pl.BlockSpec((1, N_EXPERTS, LANES), lambda i: (i, 0, 0))],
        out_shape=[jax.ShapeDtypeStruct((n, D_MODEL), F32), jax.ShapeDtypeStruct((n, LANES), F32),
                   jax.ShapeDtypeStruct((8, n), F32),
                   jax.ShapeDtypeStruct((n // tm, N_EXPERTS, LANES), F32)],
        scratch_shapes=[pltpu.VMEM((D_MODEL, D_MODEL), BF16), pltpu.VMEM((N_EXPERTS, LANES), F32)],
        compiler_params=_cparams(("arbitrary",)),
        name="merge",
    )(o_ret, pm, o_att, x2d, gn_w, w_out, ln_g, ln_b, rwt, rb_col)


def _expert(xb, wg, wu, wd):
    a = jnp.dot(xb, wg, preferred_element_type=F32)
    u = jnp.dot(xb, wu, preferred_element_type=F32)
    hid = (a * jax.nn.sigmoid(a) * u).astype(BF16)
    return jnp.dot(hid, wd, preferred_element_type=F32)


def _moe_routed_body(p1_ref, p2_ref, g1_ref, g2_ref, off_ref, cnt_ref,
                     x_ref, wg_ref, wu_ref, wd_ref, lg_ref, lb_ref, out_ref, s_ref, *, TM, R, alpha):
    i = pl.program_id(0)
    e = pl.program_id(1)
    t0 = i * TM

    @pl.when(e == 0)
    def _():
        zeros8 = jnp.zeros((8, D_MODEL), F32)
        for ee in range(N_EXPERTS):
            n = cnt_ref[i * N_EXPERTS + ee]
            s_ref[pl.ds(pl.multiple_of(off_ref[i * N_EXPERTS + ee] + (n // 8) * 8, 8), 8), :] = zeros8
        last = i * N_EXPERTS + N_EXPERTS - 1
        end = off_ref[last] + ((cnt_ref[last] + 7) // 8) * 8
        s_ref[pl.ds(pl.multiple_of(end, 8), R), :] = jnp.zeros((R, D_MODEL), F32)

        def scatter(t, c):
            row = x_ref[pl.ds(t, 1), :]
            s_ref[pl.ds(p1_ref[t0 + t], 1), :] = row
            s_ref[pl.ds(p2_ref[t0 + t], 1), :] = row
            return c

        lax.fori_loop(0, TM, scatter, 0, unroll=8)

    n = cnt_ref[i * N_EXPERTS + e]
    start = off_ref[i * N_EXPERTS + e]
    wg, wu, wd = wg_ref[0, 0], wu_ref[0, 0], wd_ref[0, 0]

    def chunk(c, carry):
        st = pl.multiple_of(start + c * R, 8)
        xs = s_ref[pl.ds(st, R), :]
        y = _expert(xs.astype(BF16), wg, wu, wd)
        mine = c * R + lax.broadcasted_iota(I32, (R, 1), 0) < n
        s_ref[pl.ds(st, R), :] = jnp.where(mine, y, xs)
        return carry

    lax.fori_loop(0, (n + R - 1) // R, chunk, 0)

    @pl.when(e == N_EXPERTS - 1)
    def _():
        def gather(t, c):
            out_ref[pl.ds(t, 1), :] = (g1_ref[t0 + t] * s_ref[pl.ds(p1_ref[t0 + t], 1), :]
                                       + g2_ref[t0 + t] * s_ref[pl.ds(p2_ref[t0 + t], 1), :])
            return c

        lax.fori_loop(0, TM, gather, 0, unroll=8)
        out_ref[...] = _layer_norm(alpha * x_ref[...] + out_ref[...], lg_ref[0], lb_ref[0])


def _moe_routed(x1, routing, w_gate, w_up, w_down, ln_g, ln_b, layer, tm, alpha):
    route, counts = routing
    n = x1.shape[0]
    dff = w_gate.shape[-1]
    nt = n // tm
    R = min(288, tm)
    group = counts.shape[0] // nt
    cnt = counts[group - 1::group, :, 0].astype(I32)
    cnt8 = (cnt + 7) // 8 * 8
    off = jnp.cumsum(cnt8, axis=1) - cnt8
    expert_ids = jnp.arange(N_EXPERTS, dtype=I32)[:, None, None]

    def row_of(e_row, r_row):
        e = e_row.astype(I32).reshape(1, nt, tm)
        start = jnp.sum(jnp.where(e == expert_ids, off.T[:, :, None], 0), axis=0)
        return start + r_row.astype(I32).reshape(nt, tm)

    p1, p2 = row_of(route[0], route[2]), row_of(route[1], route[3])
    g1, g2 = route[4], route[5]
    rows = 2 * tm + 8 * N_EXPERTS + R
    vec = pl.BlockSpec((1, 1, D_MODEL), lambda i, e, *_: (layer, 0, 0))
    grid_spec = pltpu.PrefetchScalarGridSpec(
        num_scalar_prefetch=6,
        grid=(nt, N_EXPERTS),
        in_specs=[
            pl.BlockSpec((tm, D_MODEL), lambda i, e, *_: (i, 0), pipeline_mode=pl.Buffered(1)),
            pl.BlockSpec((1, 1, D_MODEL, dff), lambda i, e, *_: (layer, e, 0, 0)),
            pl.BlockSpec((1, 1, D_MODEL, dff), lambda i, e, *_: (layer, e, 0, 0)),
            pl.BlockSpec((1, 1, dff, D_MODEL), lambda i, e, *_: (layer, e, 0, 0)),
            vec, vec,
        ],
        out_specs=pl.BlockSpec((tm, D_MODEL), lambda i, e, *_: (i, 0), pipeline_mode=pl.Buffered(1)),
        scratch_shapes=[pltpu.VMEM((rows, D_MODEL), F32)],
    )
    return pl.pallas_call(
        functools.partial(_moe_routed_body, TM=tm, R=R, alpha=alpha),
        grid_spec=grid_spec,
        out_shape=jax.ShapeDtypeStruct((n, D_MODEL), F32),
        compiler_params=_cparams(("arbitrary", "arbitrary")),
        name="moe_routed",
    )(p1.reshape(-1), p2.reshape(-1), g1.reshape(-1), g2.reshape(-1), off.reshape(-1), cnt.reshape(-1),
      x1, w_gate, w_up, w_down, ln_g, ln_b)


def _moe_body(x1_ref, gates_ref, wg_ref, wu_ref, wd_ref, lg_ref, lb_ref, out_ref, xb_ref, *, alpha):
    e = pl.program_id(1)

    @pl.when(e == 0)
    def _():
        xb_ref[...] = x1_ref[...].astype(BF16)
        out_ref[...] = jnp.zeros(out_ref.shape, F32)

    y = _expert(xb_ref[...], wg_ref[0, 0], wu_ref[0, 0], wd_ref[0, 0])
    gates = gates_ref[...]
    lane = lax.broadcasted_iota(I32, gates.shape, 1)
    g = jnp.sum(jnp.where(lane == e, gates, 0.0), axis=1, keepdims=True)
    out_ref[...] += g * y

    @pl.when(e == N_EXPERTS - 1)
    def _():
        out_ref[...] = _layer_norm(alpha * x1_ref[...] + out_ref[...], lg_ref[0], lb_ref[0])


def _moe(x1, gates, w_gate, w_up, w_down, ln_g, ln_b, layer, tm, alpha):
    n = x1.shape[0]
    dff = w_gate.shape[-1]
    vec = pl.BlockSpec((1, 1, D_MODEL), lambda i, e: (layer, 0, 0))
    return pl.pallas_call(
        functools.partial(_moe_body, alpha=alpha),
        grid=(n // tm, N_EXPERTS),
        in_specs=[
            pl.BlockSpec((tm, D_MODEL), lambda i, e: (i, 0)),
            pl.BlockSpec((tm, LANES), lambda i, e: (i, 0)),
            pl.BlockSpec((1, 1, D_MODEL, dff), lambda i, e: (layer, e, 0, 0)),
            pl.BlockSpec((1, 1, D_MODEL, dff), lambda i, e: (layer, e, 0, 0)),
            pl.BlockSpec((1, 1, dff, D_MODEL), lambda i, e: (layer, e, 0, 0)),
            vec, vec,
        ],
        out_specs=pl.BlockSpec((tm, D_MODEL), lambda i, e: (i, 0)),
        out_shape=jax.ShapeDtypeStruct((n, D_MODEL), F32),
        scratch_shapes=[pltpu.VMEM((tm, D_MODEL), BF16)],
        compiler_params=_cparams(("arbitrary", "arbitrary")),
        name="moe",
    )(x1, gates, w_gate, w_up, w_down, ln_g, ln_b)


def _rope_tables(pos):
    half = RET_D // 2
    inv = ROPE_BASE ** (-jnp.arange(half, dtype=F32) / half)
    ang = pos.astype(F32)[:, None] * inv[None, :]
    cos, sin = jnp.cos(ang), jnp.sin(ang)
    return jnp.concatenate([cos, cos], axis=1), jnp.concatenate([-sin, sin], axis=1)


def kernel(x_prompt, x_sample, cache_k, cache_v, cache_kidx, state_ret, page_table, w_in, ret_gn_w, w_out,
           ln1_g, ln1_b, router_w, router_b, w_gate, w_up, w_down, ln2_g, ln2_b):
    depth = w_in.shape[0]
    B, T, D = x_prompt.shape
    DB, TS, _ = x_sample.shape
    n_pool = cache_k.shape[1]
    W = ATT_HEADS * ATT_HD
    past = page_table.shape[1] * PAGE
    alpha = (2 * depth) ** 0.25

    w_main = w_in[:, :, :MAIN_W].astype(BF16)
    w_tail = jnp.pad(w_in[:, :, MAIN_W:], ((0, 0), (0, 0), (0, TAIL_W - (w_in.shape[2] - MAIN_W)))).astype(BF16)
    cache_k_t = cache_k.transpose(0, 1, 3, 4, 2).reshape(depth, n_pool, W, PAGE)
    cache_v_t = cache_v.transpose(0, 1, 3, 4, 2).reshape(depth, n_pool, W, PAGE)
    cache_ki_t = cache_kidx.transpose(0, 1, 3, 2)
    rwt = router_w.T
    rb_col = router_b.reshape(N_EXPERTS, 1)
    vec3 = lambda a: a.reshape(depth, 1, a.shape[-1])
    gn3, l1g, l1b, l2g, l2b = vec3(ret_gn_w), vec3(ln1_g), vec3(ln1_b), vec3(ln2_g), vec3(ln2_b)
    cos_p, sin_p = _rope_tables(jnp.arange(T))
    cos_s, sin_s = _rope_tables(past + jnp.arange(TS))
    zero_state = jnp.zeros((B, RET_HEADS, RET_D, RET_D), F32)
    C = min(LANES, T)
    wg_b, wu_b, wd_b = w_gate.astype(BF16), w_up.astype(BF16), w_down.astype(BF16)

    def block(x2d, nb, nt, o_att_fn, cosf, sinf, state0, chunk, tm_proj, tm_merge, tm_moe, moe_fn, l):
        pm, pt = _project(x2d, w_main, w_tail, l, tm_proj)
        pm3 = pm.reshape(nb, nt, MAIN_W)
        pt3 = pt.reshape(nb, nt, TAIL_W)
        o_ret, st = _retention(pm3, cosf, sinf, state0, chunk)
        o_att = o_att_fn(pm3, pt3)
        x1, gates, route, counts = _merge(o_ret.reshape(nb * nt, -1), pm, o_att.reshape(nb * nt, -1), x2d, gn3,
                                          w_out, l1g, l1b, rwt, rb_col, l, tm_merge, alpha, tm_moe // tm_merge)
        routing = (route, counts) if moe_fn is _moe_routed else gates
        x2 = moe_fn(x1, routing, wg_b, wu_b, wd_b, l2g, l2b, l, tm_moe, alpha)
        k_a = pm3[:, :, 5 * W:6 * W].reshape(nb, nt, ATT_HEADS, ATT_HD)
        v_a = pm3[:, :, 6 * W:7 * W].reshape(nb, nt, ATT_HEADS, ATT_HD)
        k_i = pt3[:, :, :IDX_DIM]
        return x2, k_a, v_a, k_i, st

    xp = x_prompt.reshape(B * T, D)
    xs = x_sample.reshape(DB * TS, D)
    outs_p, outs_s = [], []
    for l in range(depth):
        xp, *rest = block(xp, B, T, _dsa_prompt, cos_p, sin_p, zero_state, C,
                          min(2048, B * T), min(512, B * T), min(2048, B * T), _moe_routed, l)
        outs_p.append(rest)
        sample_att = functools.partial(_dsa_sample, cache_k_t=cache_k_t, cache_v_t=cache_v_t,
                                       cache_ki_t=cache_ki_t, page_table=page_table, layer=l)
        xs, *rest = block(xs, DB, TS, sample_att, cos_s, sin_s, state_ret[l], TS,
                          DB * TS, DB * TS, DB * TS, _moe, l)
        outs_s.append(rest)
    stack = lambda outs, i: jnp.stack([o[i] for o in outs])
    return (xp.reshape(B, T, D), xs.reshape(DB, TS, D),
            stack(outs_p, 0), stack(outs_p, 1), stack(outs_p, 2), stack(outs_p, 3),
            stack(outs_s, 0), stack(outs_s, 1), stack(outs_s, 2), stack(outs_s, 3))
```

```python
import functools

import numpy as np
import jax
import jax.numpy as jnp
from jax import lax
from jax.experimental import pallas as pl
from jax.experimental.pallas import tpu as pltpu

F32 = jnp.float32
BF16 = jnp.bfloat16
I32 = jnp.int32

RET_HEADS = 4
RET_D = 128
ATT_HEADS = 8
ATT_HD = 64
IDX_HEADS = 8
IDX_DIM = 64
TOPK_MAX = 256
PAGE = 128
N_EXPERTS = 16
N_GROUPS = 4
GROUP_SIZE = N_EXPERTS // N_GROUPS
ROPE_BASE = 10000.0
LN_EPS = 1e-5
D_MODEL = 1024
MAIN_W = 4096
TAIL_W = 128
LANES = 128
VMEM_LIMIT = 56 * 1024 * 1024

INT_MIN = -(2 ** 31)
NEG_INF = float("-inf")


def _log_gammas():
    h = np.arange(RET_HEADS, dtype=np.float32)
    return [float(v) for v in np.log1p(-np.exp2(-5.0 - h)).astype(np.float32)]


def _cparams(sem):
    return pltpu.CompilerParams(dimension_semantics=sem, vmem_limit_bytes=VMEM_LIMIT)


def _proj_body(x_ref, wm_ref, wt_ref, pm_ref, pt_ref, xb_ref):
    @pl.when(pl.program_id(1) == 0)
    def _():
        xb = x_ref[...].astype(BF16)
        xb_ref[...] = xb
        pt_ref[...] = jnp.dot(xb, wt_ref[0], preferred_element_type=F32)

    pm_ref[...] = jnp.dot(xb_ref[...], wm_ref[0], preferred_element_type=F32)


def _project(x2d, w_main, w_tail, layer, tm):
    n = x2d.shape[0]
    tn = 512
    return pl.pallas_call(
        _proj_body,
        grid=(n // tm, MAIN_W // tn),
        in_specs=[
            pl.BlockSpec((tm, D_MODEL), lambda i, j: (i, 0)),
            pl.BlockSpec((1, D_MODEL, tn), lambda i, j: (layer, 0, j)),
            pl.BlockSpec((1, D_MODEL, TAIL_W), lambda i, j: (layer, 0, 0)),
        ],
        out_specs=[
            pl.BlockSpec((tm, tn), lambda i, j: (i, j)),
            pl.BlockSpec((tm, TAIL_W), lambda i, j: (i, 0)),
        ],
        out_shape=[jax.ShapeDtypeStruct((n, MAIN_W), F32), jax.ShapeDtypeStruct((n, TAIL_W), F32)],
        scratch_shapes=[pltpu.VMEM((tm, D_MODEL), BF16)],
        compiler_params=_cparams(("arbitrary", "arbitrary")),
        name="proj",
    )(x2d, w_main, w_tail)


def _ret_body(q_ref, k_ref, v_ref, cos_ref, sin_ref, s0_ref, o_ref, sout_ref, st_ref, *, C, NC):
    c = pl.program_id(1)
    CP = max(C, LANES)

    @pl.when(c == 0)
    def _():
        st_ref[...] = s0_ref[0]

    def pad(a):
        if CP == C:
            return a
        return jnp.concatenate([a, jnp.zeros((CP - C, a.shape[1]), a.dtype)], axis=0)

    cosf = pad(cos_ref[...])
    sinf = pad(sin_ref[...])
    ii = lax.broadcasted_iota(I32, (CP, CP), 0)
    jj = lax.broadcasted_iota(I32, (CP, CP), 1)
    diff = (ii - jj).astype(F32)
    pos = lax.broadcasted_iota(I32, (CP, 1), 0).astype(F32)
    for h, lg in enumerate(_log_gammas()):
        sl = slice(h * RET_D, (h + 1) * RET_D)
        q = pad(q_ref[0, :, sl])
        k = pad(k_ref[0, :, sl])
        vb = pad(v_ref[0, :, sl]).astype(BF16)
        q = q * cosf + pltpu.roll(q, RET_D // 2, 1) * sinf
        k = (k * cosf + pltpu.roll(k, RET_D // 2, 1) * sinf) * (RET_D ** -0.5)
        decay = jnp.where(diff >= 0, jnp.exp(lg * jnp.maximum(diff, 0.0)), 0.0)
        qb = q.astype(BF16)
        scores = lax.dot_general(qb, k.astype(BF16), (((1,), (1,)), ((), ())),
                                 preferred_element_type=F32) * decay
        inner = jnp.dot(scores.astype(BF16), vb, preferred_element_type=F32)
        st = st_ref[h]
        cross = jnp.dot(qb, st.astype(BF16), preferred_element_type=F32) * jnp.exp(lg * (pos + 1.0))
        o_ref[0, :, sl] = (inner + cross)[:C]
        kd = k * jnp.exp(lg * (C - 1.0 - pos))
        st_ref[h] = st * float(np.exp(np.float32(lg) * np.float32(C))) + jnp.dot(
            kd.T.astype(BF16), vb, preferred_element_type=F32)

    @pl.when(c == NC - 1)
    def _():
        sout_ref[0] = st_ref[...]


def _retention(pm3, cosf, sinf, state0, C):
    B, T, _ = pm3.shape
    NC = T // C
    W = RET_HEADS * RET_D
    return pl.pallas_call(
        functools.partial(_ret_body, C=C, NC=NC),
        grid=(B, NC),
        in_specs=[
            pl.BlockSpec((1, C, W), lambda b, c: (b, c, 0)),
            pl.BlockSpec((1, C, W), lambda b, c: (b, c, 1)),
            pl.BlockSpec((1, C, W), lambda b, c: (b, c, 2)),
            pl.BlockSpec((C, RET_D), lambda b, c: (c, 0)),
            pl.BlockSpec((C, RET_D), lambda b, c: (c, 0)),
            pl.BlockSpec((1, RET_HEADS, RET_D, RET_D), lambda b, c: (b, 0, 0, 0)),
        ],
        out_specs=[
            pl.BlockSpec((1, C, W), lambda b, c: (b, c, 0)),
            pl.BlockSpec((1, RET_HEADS, RET_D, RET_D), lambda b, c: (b, 0, 0, 0)),
        ],
        out_shape=[jax.ShapeDtypeStruct((B, T, W), F32),
                   jax.ShapeDtypeStruct((B, RET_HEADS, RET_D, RET_D), F32)],
        scratch_shapes=[pltpu.VMEM((RET_HEADS, RET_D, RET_D), F32)],
        compiler_params=_cparams(("arbitrary", "arbitrary")),
        name="retention",
    )(pm3, pm3, pm3, cosf, sinf, state0)


def _sortable(x):
    bits = pltpu.bitcast(x + 0.0, I32)
    return bits ^ ((bits >> 31) & 0x7FFFFFFF)


def _kth_largest(count_ge, shape, kf):
    c0 = count_ge(jnp.zeros(shape, I32))
    tau = jnp.where(c0 >= kf, 0, INT_MIN).astype(I32)

    def body(it, tau):
        cand = tau | (jnp.int32(1) << (30 - it))
        return jnp.where(count_ge(cand) >= kf, cand, tau)

    return lax.fori_loop(0, 31, body, tau)


def _tie_cutoff(count_lt, shape, need, nbits):
    def body(it, m):
        cand = m + (jnp.int32(1) << (nbits - 1 - it))
        return jnp.where(count_lt(cand) < need, cand, m)

    return lax.fori_loop(0, nbits, body, jnp.zeros(shape, I32))


def _col_reduce(x, op):
    S, Q = x.shape
    if S > LANES:
        x = op(x.reshape(S // LANES, LANES, Q), axis=0)
    return op(x, axis=0, keepdims=True)


def _dsa_prompt_block(i, S, qa_ref, qi_ref, ptq_ref, o_ref, kb_ref, vt_ref, kib_ref, keys_ref, bias_ref,
                      *, QB, ksel):
    RC = 256 if S % 256 == 0 else LANES
    qiT = qi_ref[0].T
    ptT = ptq_ref[0].T
    zeros64 = jnp.zeros((IDX_DIM, QB), BF16)
    rhs_idx = jnp.concatenate(
        [jnp.concatenate([qiT[h * IDX_DIM:(h + 1) * IDX_DIM].astype(BF16), zeros64], axis=0)
         for h in range(IDX_HEADS)], axis=1)
    w_rows = [ptT[IDX_DIM + h:IDX_DIM + h + 1, :] * (IDX_HEADS ** -0.5) * (IDX_DIM ** -0.5)
              for h in range(IDX_HEADS)]
    t_pos = i * QB + lax.broadcasted_iota(I32, (RC, QB), 1)
    for r in range(S // RC):
        d = jnp.dot(kib_ref[r * RC:(r + 1) * RC, :], rhs_idx, preferred_element_type=F32)
        acc = jnp.maximum(d[:, 0:QB], 0.0) * w_rows[0]
        for h in range(1, IDX_HEADS):
            acc = acc + jnp.maximum(d[:, h * QB:(h + 1) * QB], 0.0) * w_rows[h]
        s_pos = r * RC + lax.broadcasted_iota(I32, (RC, QB), 0)
        keys_ref[r * RC:(r + 1) * RC, :] = _sortable(jnp.where(s_pos <= t_pos, acc, NEG_INF))

    kf = float(ksel)
    NT = S // LANES

    def count(pred):
        acc = jnp.zeros((LANES, QB), F32)
        for r in range(NT):
            acc = jnp.where(pred(keys_ref[r * LANES:(r + 1) * LANES, :], r * LANES), acc + 1.0, acc)
        return jnp.sum(acc, axis=0, keepdims=True)

    tau = _kth_largest(lambda cand: count(lambda k, _: k >= cand), (1, QB), kf)
    row = lax.broadcasted_iota(I32, (LANES, QB), 0)
    t_row = i * QB + lax.broadcasted_iota(I32, (LANES, QB), 1)
    need = kf - count(lambda k, _: k > tau)
    n_eq = count(lambda k, r0: (k == tau) & (r0 + row <= t_row))
    any_split = jnp.max(jnp.where(n_eq > need, 1.0, 0.0)) > 0.0

    def slow():
        return _tie_cutoff(lambda m: count(lambda k, r0: (k == tau) & (r0 + row <= t_row) & (r0 + row < m)),
                           (1, QB), need, int(S).bit_length())

    mcut = lax.cond(any_split, slow, lambda: jnp.full((1, QB), S, I32))
    for r in range(NT):
        k = keys_ref[r * LANES:(r + 1) * LANES, :]
        s_pos = r * LANES + row
        sel = (s_pos <= t_row) & ((k > tau) | ((k == tau) & (s_pos <= mcut)))
        bias_ref[r * LANES:(r + 1) * LANES, :] = jnp.where(sel, 0.0, NEG_INF)

    qaT = qa_ref[0].T * (ATT_HD ** -0.5)
    zq = jnp.zeros((ATT_HD, QB), BF16)
    outs = []
    for j in range(ATT_HEADS // 2):
        r0 = 2 * j * ATT_HD
        top = jnp.concatenate([qaT[r0:r0 + ATT_HD].astype(BF16), zq], axis=1)
        bot = jnp.concatenate([zq, qaT[r0 + ATT_HD:r0 + 2 * ATT_HD].astype(BF16)], axis=1)
        rhs = jnp.concatenate([top, bot], axis=0)
        sT = jnp.dot(kb_ref[0:S, r0:r0 + 2 * ATT_HD], rhs, preferred_element_type=F32)
        for u in range(2):
            h = 2 * j + u
            s = sT[:, u * QB:(u + 1) * QB] + bias_ref[0:S, :]
            m = _col_reduce(s, jnp.max)
            p = jnp.exp(s - m)
            l = _col_reduce(p, jnp.sum)
            oT = jnp.dot(vt_ref[h * ATT_HD:(h + 1) * ATT_HD, 0:S], p.astype(BF16),
                         preferred_element_type=F32)
            outs.append(oT / l)
    o_ref[0] = jnp.concatenate(outs, axis=0).T


def _dsa_prompt_body(qa_ref, k_ref, v_ref, qi_ref, pt_ref, ptq_ref, o_ref,
                     kb_ref, vt_ref, kib_ref, keys_ref, bias_ref, *, T, QB, ksel, NV):
    i = pl.program_id(1)

    @pl.when(i == 0)
    def _():
        kb_ref[...] = k_ref[0].astype(BF16)
        vt_ref[...] = v_ref[0].T.astype(BF16)
        kib_ref[...] = pt_ref[0].astype(BF16)

    per = (T // QB) // NV
    for g in range(NV):
        @pl.when(i // per == g)
        def _(g=g):
            _dsa_prompt_block(i, (g + 1) * per * QB, qa_ref, qi_ref, ptq_ref, o_ref, kb_ref, vt_ref, kib_ref,
                              keys_ref, bias_ref, QB=QB, ksel=ksel)


def _dsa_prompt(pm3, pt3):
    B, T, _ = pm3.shape
    QB = min(2 * LANES, T)
    ksel = min(TOPK_MAX, T // 4)
    W = ATT_HEADS * ATT_HD
    NV = min(8, T // QB)
    return pl.pallas_call(
        functools.partial(_dsa_prompt_body, T=T, QB=QB, ksel=ksel, NV=NV),
        grid=(B, T // QB),
        in_specs=[
            pl.BlockSpec((1, QB, W), lambda b, i: (b, i, 4)),
            pl.BlockSpec((1, T, W), lambda b, i: (b, 0, 5)),
            pl.BlockSpec((1, T, W), lambda b, i: (b, 0, 6)),
            pl.BlockSpec((1, QB, W), lambda b, i: (b, i, 7)),
            pl.BlockSpec((1, T, TAIL_W), lambda b, i: (b, 0, 0)),
            pl.BlockSpec((1, QB, TAIL_W), lambda b, i: (b, i, 0)),
        ],
        out_specs=pl.BlockSpec((1, QB, W), lambda b, i: (b, i, 0)),
        out_shape=jax.ShapeDtypeStruct((B, T, W), F32),
        scratch_shapes=[pltpu.VMEM((T, W), BF16), pltpu.VMEM((W, T), BF16),
                        pltpu.VMEM((T, TAIL_W), BF16), pltpu.VMEM((T, QB), I32),
                        pltpu.VMEM((T, QB), F32)],
        compiler_params=_cparams(("arbitrary", "arbitrary")),
        name="dsa_prompt",
    )(pm3, pm3, pm3, pm3, pt3, pt3)


PAGES_PER_STEP = 16
IDX_PAGES_PER_STEP = 64


def _idx_sample_body(ptab_ref, *refs, P):
    del ptab_ref
    page_refs = refs[:P]
    qi_ref, w_ref, kin_ref, sc_ref, scn_ref = refs[P:]
    q = qi_ref[0].astype(BF16)
    w = w_ref[0] * (IDX_HEADS ** -0.5) * (IDX_DIM ** -0.5)
    nq = q.shape[0] // IDX_HEADS

    def scores(keys_t):
        d = jnp.dot(q, keys_t.astype(BF16), preferred_element_type=F32)
        r = jnp.maximum(d, 0.0) * w
        acc = r[0:nq]
        for h in range(1, IDX_HEADS):
            acc = acc + r[h * nq:(h + 1) * nq]
        return acc

    sc_ref[0] = scores(jnp.concatenate([r[0, 0] for r in page_refs], axis=1))

    @pl.when(pl.program_id(1) == 0)
    def _():
        scn_ref[0] = scores(kin_ref[0])


def _idx_sample(page_table, cache_ki_t, layer, qi_flat, w_col, ki_new_t):
    DB, n_pages = page_table.shape
    P = int(np.gcd(n_pages, IDX_PAGES_PER_STEP))
    NCH = n_pages // P
    nq = qi_flat.shape[1] // IDX_HEADS

    def page_spec(p):
        return pl.BlockSpec((1, 1, IDX_DIM, PAGE), lambda b, c, pt: (layer, pt[b, c * P + p], 0, 0))

    grid_spec = pltpu.PrefetchScalarGridSpec(
        num_scalar_prefetch=1,
        grid=(DB, NCH),
        in_specs=[page_spec(p) for p in range(P)] + [
            pl.BlockSpec((1, IDX_HEADS * nq, IDX_DIM), lambda b, c, pt: (b, 0, 0)),
            pl.BlockSpec((1, IDX_HEADS * nq, 1), lambda b, c, pt: (b, 0, 0)),
            pl.BlockSpec((1, IDX_DIM, PAGE), lambda b, c, pt: (b, 0, 0)),
        ],
        out_specs=[
            pl.BlockSpec((1, nq, P * PAGE), lambda b, c, pt: (b, 0, c)),
            pl.BlockSpec((1, nq, PAGE), lambda b, c, pt: (b, 0, 0)),
        ],
    )
    return pl.pallas_call(
        functools.partial(_idx_sample_body, P=P),
        grid_spec=grid_spec,
        out_shape=[jax.ShapeDtypeStruct((DB, nq, NCH * P * PAGE), F32),
                   jax.ShapeDtypeStruct((DB, nq, PAGE), F32)],
        compiler_params=_cparams(("arbitrary", "arbitrary")),
        name="idx_sample",
    )(page_table, *([cache_ki_t] * P), qi_flat, w_col, ki_new_t)


def _select_sample_body(sc_ref, scn_ref, bm_ref, bn_ref, keys_ref, *, GB, nq, past, ksel):
    R = GB * nq
    S = past + PAGE
    NT = S // LANES
    t_of_row = lax.broadcasted_iota(I32, (GB, nq, LANES), 1).reshape(R, LANES)
    lane = lax.broadcasted_iota(I32, (R, LANES), 1)
    new_valid = lane <= t_of_row
    keys_ref[:, :past] = _sortable(sc_ref[...].reshape(R, past))
    keys_ref[:, past:] = _sortable(jnp.where(new_valid, scn_ref[...].reshape(R, PAGE), NEG_INF))
    kf = float(ksel)

    def count(pred):
        acc = jnp.zeros((R, LANES), F32)
        for c in range(NT):
            ok = pred(keys_ref[:, c * LANES:(c + 1) * LANES], c * LANES, new_valid if c == NT - 1 else None)
            acc = jnp.where(ok, acc + 1.0, acc)
        return jnp.sum(acc, axis=1, keepdims=True)

    def tied(k, valid, tau):
        return (k == tau) if valid is None else (k == tau) & valid

    tau = _kth_largest(lambda cand: count(lambda k, c0, v: k >= cand), (R, 1), kf)
    need = kf - count(lambda k, c0, v: k > tau)
    n_eq = count(lambda k, c0, v: tied(k, v, tau))
    any_split = jnp.max(jnp.where(n_eq > need, 1.0, 0.0)) > 0.0

    def slow():
        return _tie_cutoff(lambda m: count(lambda k, c0, v: tied(k, v, tau) & (c0 + lane < m)),
                           (R, 1), need, int(S).bit_length())

    mcut = lax.cond(any_split, slow, lambda: jnp.full((R, 1), S, I32))
    for c in range(NT):
        k = keys_ref[:, c * LANES:(c + 1) * LANES]
        valid = new_valid if c == NT - 1 else None
        sel = (k > tau) | (tied(k, valid, tau) & (c * LANES + lane <= mcut))
        if valid is not None:
            sel = sel & valid
        bias = jnp.where(sel, 0.0, NEG_INF).reshape(GB, nq, LANES)
        if c == NT - 1:
            bn_ref[...] = bias
        else:
            bm_ref[:, :, c * LANES:(c + 1) * LANES] = bias


def _select_sample(sc, scn, ksel):
    DB, nq, past = sc.shape
    GB = int(np.gcd(DB, 8))
    return pl.pallas_call(
        functools.partial(_select_sample_body, GB=GB, nq=nq, past=past, ksel=ksel),
        grid=(DB // GB,),
        in_specs=[pl.BlockSpec((GB, nq, past), lambda b: (b, 0, 0)),
                  pl.BlockSpec((GB, nq, PAGE), lambda b: (b, 0, 0))],
        out_specs=[pl.BlockSpec((GB, nq, past), lambda b: (b, 0, 0)),
                   pl.BlockSpec((GB, nq, PAGE), lambda b: (b, 0, 0))],
        out_shape=[jax.ShapeDtypeStruct((DB, nq, past), F32),
                   jax.ShapeDtypeStruct((DB, nq, PAGE), F32)],
        scratch_shapes=[pltpu.VMEM((GB * nq, past + PAGE), I32)],
        compiler_params=_cparams(("arbitrary",)),
        name="select_sample",
    )(sc, scn)


def _att_sample_body(ptab_ref, *refs, P, NCH, nq):
    del ptab_ref
    k_refs = refs[:P]
    v_refs = refs[P:2 * P]
    q_ref, bm_ref, bn_ref, kn_ref, vn_ref, o_ref, m_ref, l_ref, acc_ref = refs[2 * P:]
    c = pl.program_id(1)

    @pl.when(c == 0)
    def _():
        m_ref[...] = jnp.full(m_ref.shape, NEG_INF, F32)
        l_ref[...] = jnp.zeros(l_ref.shape, F32)
        acc_ref[...] = jnp.zeros(acc_ref.shape, F32)

    q = (q_ref[0] * (ATT_HD ** -0.5)).astype(BF16)

    def update(k_t, v_t, bias):
        s = jnp.dot(q, k_t.astype(BF16), preferred_element_type=F32)
        s = s + jnp.concatenate([bias] * ATT_HEADS, axis=0)
        m_old = m_ref[...]
        m_new = jnp.maximum(m_old, jnp.max(s, axis=1, keepdims=True))
        m_safe = jnp.where(m_new == NEG_INF, 0.0, m_new)
        alpha = jnp.exp(m_old - m_safe)
        p = jnp.exp(s - m_safe)
        l_ref[...] = l_ref[...] * alpha + jnp.sum(p, axis=1, keepdims=True)
        acc_ref[...] = acc_ref[...] * alpha + lax.dot_general(
            p.astype(BF16), v_t.astype(BF16), (((1,), (1,)), ((), ())), preferred_element_type=F32)
        m_ref[...] = m_new

    update(jnp.concatenate([r[0, 0] for r in k_refs], axis=1),
           jnp.concatenate([r[0, 0] for r in v_refs], axis=1), bm_ref[0])

    @pl.when(c == NCH - 1)
    def _():
        update(kn_ref[0], vn_ref[0], bn_ref[0])
        o = acc_ref[...] / l_ref[...]
        head_of_lane = lax.broadcasted_iota(I32, (nq, ATT_HEADS * ATT_HD), 1) // ATT_HD
        out = jnp.zeros((nq, ATT_HEADS * ATT_HD), F32)
        for h in range(ATT_HEADS):
            out = out + jnp.where(head_of_lane == h, o[h * nq:(h + 1) * nq], 0.0)
        o_ref[0] = out


def _att_sample(page_table, cache_k_t, cache_v_t, layer, q_bd, bias_main, bias_new, k_new_t, v_new_t):
    DB, n_pages = page_table.shape
    P = int(np.gcd(n_pages, PAGES_PER_STEP))
    NCH = n_pages // P
    nq = bias_main.shape[1]
    W = ATT_HEADS * ATT_HD

    def page_spec(p):
        return pl.BlockSpec((1, 1, W, PAGE), lambda b, c, pt: (layer, pt[b, c * P + p], 0, 0))

    grid_spec = pltpu.PrefetchScalarGridSpec(
        num_scalar_prefetch=1,
        grid=(DB, NCH),
        in_specs=[page_spec(p) for p in range(P)] + [page_spec(p) for p in range(P)] + [
            pl.BlockSpec((1, ATT_HEADS * nq, W), lambda b, c, pt: (b, 0, 0)),
            pl.BlockSpec((1, nq, P * PAGE), lambda b, c, pt: (b, 0, c)),
            pl.BlockSpec((1, nq, PAGE), lambda b, c, pt: (b, 0, 0)),
            pl.BlockSpec((1, W, PAGE), lambda b, c, pt: (b, 0, 0)),
            pl.BlockSpec((1, W, PAGE), lambda b, c, pt: (b, 0, 0)),
        ],
        out_specs=pl.BlockSpec((1, nq, W), lambda b, c, pt: (b, 0, 0)),
        scratch_shapes=[pltpu.VMEM((ATT_HEADS * nq, 1), F32), pltpu.VMEM((ATT_HEADS * nq, 1), F32),
                        pltpu.VMEM((ATT_HEADS * nq, W), F32)],
    )
    return pl.pallas_call(
        functools.partial(_att_sample_body, P=P, NCH=NCH, nq=nq),
        grid_spec=grid_spec,
        out_shape=jax.ShapeDtypeStruct((DB, nq, W), F32),
        compiler_params=_cparams(("arbitrary", "arbitrary")),
        name="att_sample",
    )(page_table, *([cache_k_t] * P), *([cache_v_t] * P), q_bd, bias_main, bias_new, k_new_t, v_new_t)


def _dsa_sample(pm3, pt3, cache_k_t, cache_v_t, cache_ki_t, page_table, layer):
    DB, nq, _ = pm3.shape
    W = ATT_HEADS * ATT_HD
    past = page_table.shape[1] * PAGE
    ksel = min(TOPK_MAX, (past + nq) // 4)
    q_a = pm3[:, :, 4 * W:5 * W].reshape(DB, nq, ATT_HEADS, ATT_HD)
    q_i = pm3[:, :, 7 * W:8 * W].reshape(DB, nq, IDX_HEADS, IDX_DIM)
    qi_flat = q_i.transpose(0, 2, 1, 3).reshape(DB, IDX_HEADS * nq, IDX_DIM)
    w_col = pt3[:, :, IDX_DIM:IDX_DIM + IDX_HEADS].transpose(0, 2, 1).reshape(DB, IDX_HEADS * nq, 1)
    q_bd = jnp.einsum("bthd,hg->bhtgd", q_a, jnp.eye(ATT_HEADS, dtype=F32)).reshape(DB, ATT_HEADS * nq, W)

    def new_t(a):
        return jnp.pad(a.transpose(0, 2, 1), ((0, 0), (0, 0), (0, PAGE - nq)))

    sc, scn = _idx_sample(page_table, cache_ki_t, layer, qi_flat, w_col, new_t(pt3[:, :, :IDX_DIM]))
    bias_main, bias_new = _select_sample(sc, scn, ksel)
    return _att_sample(page_table, cache_k_t, cache_v_t, layer, q_bd, bias_main, bias_new,
                       new_t(pm3[:, :, 5 * W:6 * W]), new_t(pm3[:, :, 6 * W:7 * W]))


def _layer_norm(y, g, b):
    mu = jnp.mean(y, axis=-1, keepdims=True)
    var = jnp.mean(jnp.square(y - mu), axis=-1, keepdims=True)
    return (y - mu) * lax.rsqrt(var + LN_EPS) * g + b


def _top2_of4(b0, b1, b2, b3):
    hi01, lo01 = jnp.maximum(b0, b1), jnp.minimum(b0, b1)
    hi23, lo23 = jnp.maximum(b2, b3), jnp.minimum(b2, b3)
    return jnp.maximum(hi01, hi23), jnp.maximum(jnp.minimum(hi01, hi23), jnp.maximum(lo01, lo23))


def _route_rows(aff, biased):
    gscore = []
    for g in range(N_GROUPS):
        m1, m2 = _top2_of4(*biased[g * GROUP_SIZE:(g + 1) * GROUP_SIZE])
        gscore.append(m1 + m2)
    gmax = functools.reduce(jnp.maximum, gscore)
    taken = jnp.zeros_like(gmax, dtype=jnp.bool_)
    gsel = []
    for g in range(N_GROUPS):
        hit = (gscore[g] == gmax) & jnp.logical_not(taken)
        gsel.append(hit)
        taken = taken | hit
    cand, caff = [], []
    for k in range(GROUP_SIZE):
        c = jnp.full_like(gmax, NEG_INF)
        a = jnp.zeros_like(gmax)
        for g in range(N_GROUPS):
            c = jnp.where(gsel[g], biased[g * GROUP_SIZE + k], c)
            a = jnp.where(gsel[g], aff[g * GROUP_SIZE + k], a)
        cand.append(c)
        caff.append(a)
    best = functools.reduce(jnp.maximum, cand)
    taken = jnp.zeros_like(taken)
    first = []
    for k in range(GROUP_SIZE):
        hit = (cand[k] == best) & jnp.logical_not(taken)
        first.append(hit)
        taken = taken | hit
    rest = [jnp.where(first[k], NEG_INF, cand[k]) for k in range(GROUP_SIZE)]
    best2 = functools.reduce(jnp.maximum, rest)
    taken = jnp.zeros_like(taken)
    second = []
    for k in range(GROUP_SIZE):
        hit = (rest[k] == best2) & jnp.logical_not(first[k]) & jnp.logical_not(taken)
        second.append(hit)
        taken = taken | hit
    a1 = functools.reduce(jnp.add, [jnp.where(first[k], caff[k], 0.0) for k in range(GROUP_SIZE)])
    a2 = functools.reduce(jnp.add, [jnp.where(second[k], caff[k], 0.0) for k in range(GROUP_SIZE)])
    tot = a1 + a2
    w1, w2 = a1 / tot, a2 / tot
    gates, chosen = [], []
    for g in range(N_GROUPS):
        for k in range(GROUP_SIZE):
            gates.append(jnp.where(gsel[g] & first[k], w1, jnp.where(gsel[g] & second[k], w2, 0.0)))
            chosen.append(jnp.where(gsel[g] & (first[k] | second[k]), 1.0, 0.0))
    return gates, chosen


def _merge_body(oret_ref, g_ref, oatt_ref, x_ref, gn_ref, wout_ref, lg_ref, lb_ref, rwt_ref, rb_ref,
                x1_ref, gates_ref, route_ref, cnt_ref, wob_ref, carry_ref, *, alpha, group):
    @pl.when(pl.program_id(0) == 0)
    def _():
        wob_ref[...] = wout_ref[0].astype(BF16)

    @pl.when(pl.program_id(0) % group == 0)
    def _():
        carry_ref[...] = jnp.zeros(carry_ref.shape, F32)

    parts = []
    for h in range(RET_HEADS):
        sl = slice(h * RET_D, (h + 1) * RET_D)
        o = oret_ref[:, sl]
        mu = jnp.mean(o, axis=-1, keepdims=True)
        var = jnp.mean(jnp.square(o - mu), axis=-1, keepdims=True)
        r = (o - mu) * lax.rsqrt(var + LN_EPS) * gn_ref[0, :, sl]
        g = g_ref[:, sl]
        parts.append(g * jax.nn.sigmoid(g) * r)
    cat = jnp.concatenate(parts + [oatt_ref[...]], axis=1).astype(BF16)
    mix = jnp.dot(cat, wob_ref[...], preferred_element_type=F32)
    x1 = _layer_norm(alpha * x_ref[...] + mix, lg_ref[0], lb_ref[0])
    x1_ref[...] = x1

    logits = lax.dot_general(rwt_ref[...], x1, (((1,), (1,)), ((), ())),
                             precision=lax.Precision.HIGHEST, preferred_element_type=F32)
    aff = jax.nn.sigmoid(logits)
    biased = aff + rb_ref[...]
    gates, chosen = _route_rows([aff[e:e + 1] for e in range(N_EXPERTS)],
                                [biased[e:e + 1] for e in range(N_EXPERTS)])
    tm = x1.shape[0]
    gt = jnp.concatenate(gates + chosen + [jnp.zeros((LANES - 2 * N_EXPERTS, tm), F32)], axis=0)
    gates_ref[...] = gt.T

    cmat = jnp.concatenate(chosen, axis=0)
    upper = (lax.broadcasted_iota(I32, (tm, tm), 0) <= lax.broadcasted_iota(I32, (tm, tm), 1))
    csum = jnp.dot(cmat.astype(BF16), jnp.where(upper, 1.0, 0.0).astype(BF16),
                   preferred_element_type=F32)
    carry = carry_ref[...]
    rank = carry[:, 0:1] + csum - 1.0
    carry = carry + csum[:, tm - 1:tm]
    carry_ref[...] = carry
    cnt_ref[0] = carry
    zero = jnp.zeros((1, tm), F32)
    seen, e1, e2, r1, r2, g1, g2 = zero, zero, zero, zero, zero, zero, zero
    for e in range(N_EXPERTS):
        hit = chosen[e] > 0.5
        first = hit & (seen == 0.0)
        second = hit & (seen == 1.0)
        e1 = jnp.where(first, float(e), e1)
        e2 = jnp.where(second, float(e), e2)
        r1 = jnp.where(first, rank[e:e + 1], r1)
        r2 = jnp.where(second, rank[e:e + 1], r2)
        g1 = jnp.where(first, gates[e], g1)
        g2 = jnp.where(second, gates[e], g2)
        seen = seen + chosen[e]
    route_ref[...] = jnp.concatenate([e1, e2, r1, r2, g1, g2, zero, zero], axis=0)


def _merge(o_ret, pm, o_att, x2d, gn_w, w_out, ln_g, ln_b, rwt, rb_col, layer, tm, alpha, group):
    n = x2d.shape[0]
    W = RET_HEADS * RET_D
    vec = lambda width: pl.BlockSpec((1, 1, width), lambda i: (layer, 0, 0))
    return pl.pallas_call(
        functools.partial(_merge_body, alpha=alpha, group=group),
        grid=(n // tm,),
        in_specs=[
            pl.BlockSpec((tm, W), lambda i: (i, 0)),
            pl.BlockSpec((tm, W), lambda i: (i, 3)),
            pl.BlockSpec((tm, W), lambda i: (i, 0)),
            pl.BlockSpec((tm, D_MODEL), lambda i: (i, 0)),
            vec(W),
            pl.BlockSpec((1, D_MODEL, D_MODEL), lambda i: (layer, 0, 0)),
            vec(D_MODEL), vec(D_MODEL),
            pl.BlockSpec((N_EXPERTS, D_MODEL), lambda i: (0, 0)),
            pl.BlockSpec((N_EXPERTS, 1), lambda i: (0, 0)),
        ],
        out_specs=[pl.BlockSpec((tm, D_MODEL), lambda i: (i, 0)),
                   pl.BlockSpec((tm, LANES), lambda i: (i, 0)),
                   pl.BlockSpec((8, tm), lambda i: (0, i)),
                   pl.BlockSpec((1, N_EXPERTS, LANES), lambda i: (i, 0, 0))],
        out_shape=[jax.ShapeDtypeStruct((n, D_MODEL), F32), jax.ShapeDtypeStruct((n, LANES), F32),
                   jax.ShapeDtypeStruct((8, n), F32),
                   jax.ShapeDtypeStruct((n // tm, N_EXPERTS, LANES), F32)],
        scratch_shapes=[pltpu.VMEM((D_MODEL, D_MODEL), BF16), pltpu.VMEM((N_EXPERTS, LANES), F32)],
        compiler_params=_cparams(("arbitrary",)),
        name="merge",
    )(o_ret, pm, o_att, x2d, gn_w, w_out, ln_g, ln_b, rwt, rb_col)


def _expert(xb, wg, wu, wd):
    a = jnp.dot(xb, wg, preferred_element_type=F32)
    u = jnp.dot(xb, wu, preferred_element_type=F32)
    hid = (a * jax.nn.sigmoid(a) * u).astype(BF16)
    return jnp.dot(hid, wd, preferred_element_type=F32)


def _moe_routed_body(p1_ref, p2_ref, g1_ref, g2_ref, off_ref, cnt_ref,
                     x_ref, wg_ref, wu_ref, wd_ref, lg_ref, lb_ref, out_ref, s_ref, *, TM, R, alpha):
    i = pl.program_id(0)
    e = pl.program_id(1)
    t0 = i * TM

    @pl.when(e == 0)
    def _():
        zeros8 = jnp.zeros((8, D_MODEL), F32)
        for ee in range(N_EXPERTS):
            n = cnt_ref[i * N_EXPERTS + ee]
            s_ref[pl.ds(pl.multiple_of(off_ref[i * N_EXPERTS + ee] + (n // 8) * 8, 8), 8), :] = zeros8
        last = i * N_EXPERTS + N_EXPERTS - 1
        end = off_ref[last] + ((cnt_ref[last] + 7) // 8) * 8
        s_ref[pl.ds(pl.multiple_of(end, 8), R), :] = jnp.zeros((R, D_MODEL), F32)

        def scatter(t, c):
            row = x_ref[pl.ds(t, 1), :]
            s_ref[pl.ds(p1_ref[t0 + t], 1), :] = row
            s_ref[pl.ds(p2_ref[t0 + t], 1), :] = row
            return c

        lax.fori_loop(0, TM, scatter, 0, unroll=8)

    n = cnt_ref[i * N_EXPERTS + e]
    start = off_ref[i * N_EXPERTS + e]
    wg, wu, wd = wg_ref[0, 0], wu_ref[0, 0], wd_ref[0, 0]

    def chunk(c, carry):
        st = pl.multiple_of(start + c * R, 8)
        xs = s_ref[pl.ds(st, R), :]
        y = _expert(xs.astype(BF16), wg, wu, wd)
        mine = c * R + lax.broadcasted_iota(I32, (R, 1), 0) < n
        s_ref[pl.ds(st, R), :] = jnp.where(mine, y, xs)
        return carry

    lax.fori_loop(0, (n + R - 1) // R, chunk, 0)

    @pl.when(e == N_EXPERTS - 1)
    def _():
        def gather(t, c):
            out_ref[pl.ds(t, 1), :] = (g1_ref[t0 + t] * s_ref[pl.ds(p1_ref[t0 + t], 1), :]
                                       + g2_ref[t0 + t] * s_ref[pl.ds(p2_ref[t0 + t], 1), :])
            return c

        lax.fori_loop(0, TM, gather, 0, unroll=8)
        out_ref[...] = _layer_norm(alpha * x_ref[...] + out_ref[...], lg_ref[0], lb_ref[0])


def _moe_routed(x1, routing, w_gate, w_up, w_down, ln_g, ln_b, layer, tm, alpha):
    route, counts = routing
    n = x1.shape[0]
    dff = w_gate.shape[-1]
    nt = n // tm
    R = min(288, tm)
    group = counts.shape[0] // nt
    cnt = counts[group - 1::group, :, 0].astype(I32)
    cnt8 = (cnt + 7) // 8 * 8
    off = jnp.cumsum(cnt8, axis=1) - cnt8
    expert_ids = jnp.arange(N_EXPERTS, dtype=I32)[:, None, None]

    def row_of(e_row, r_row):
        e = e_row.astype(I32).reshape(1, nt, tm)
        start = jnp.sum(jnp.where(e == expert_ids, off.T[:, :, None], 0), axis=0)
        return start + r_row.astype(I32).reshape(nt, tm)

    p1, p2 = row_of(route[0], route[2]), row_of(route[1], route[3])
    g1, g2 = route[4], route[5]
    rows = 2 * tm + 8 * N_EXPERTS + R
    vec = pl.BlockSpec((1, 1, D_MODEL), lambda i, e, *_: (layer, 0, 0))
    grid_spec = pltpu.PrefetchScalarGridSpec(
        num_scalar_prefetch=6,
        grid=(nt, N_EXPERTS),
        in_specs=[
            pl.BlockSpec((tm, D_MODEL), lambda i, e, *_: (i, 0), pipeline_mode=pl.Buffered(1)),
            pl.BlockSpec((1, 1, D_MODEL, dff), lambda i, e, *_: (layer, e, 0, 0)),
            pl.BlockSpec((1, 1, D_MODEL, dff), lambda i, e, *_: (layer, e, 0, 0)),
            pl.BlockSpec((1, 1, dff, D_MODEL), lambda i, e, *_: (layer, e, 0, 0)),
            vec, vec,
        ],
        out_specs=pl.BlockSpec((tm, D_MODEL), lambda i, e, *_: (i, 0), pipeline_mode=pl.Buffered(1)),
        scratch_shapes=[pltpu.VMEM((rows, D_MODEL), F32)],
    )
    return pl.pallas_call(
        functools.partial(_moe_routed_body, TM=tm, R=R, alpha=alpha),
        grid_spec=grid_spec,
        out_shape=jax.ShapeDtypeStruct((n, D_MODEL), F32),
        compiler_params=_cparams(("arbitrary", "arbitrary")),
        name="moe_routed",
    )(p1.reshape(-1), p2.reshape(-1), g1.reshape(-1), g2.reshape(-1), off.reshape(-1), cnt.reshape(-1),
      x1, w_gate, w_up, w_down, ln_g, ln_b)


def _moe_body(x1_ref, gates_ref, wg_ref, wu_ref, wd_ref, lg_ref, lb_ref, out_ref, xb_ref, *, alpha):
    e = pl.program_id(1)

    @pl.when(e == 0)
    def _():
        xb_ref[...] = x1_ref[...].astype(BF16)
        out_ref[...] = jnp.zeros(out_ref.shape, F32)

    y = _expert(xb_ref[...], wg_ref[0, 0], wu_ref[0, 0], wd_ref[0, 0])
    gates = gates_ref[...]
    lane = lax.broadcasted_iota(I32, gates.shape, 1)
    g = jnp.sum(jnp.where(lane == e, gates, 0.0), axis=1, keepdims=True)
    out_ref[...] += g * y

    @pl.when(e == N_EXPERTS - 1)
    def _():
        out_ref[...] = _layer_norm(alpha * x1_ref[...] + out_ref[...], lg_ref[0], lb_ref[0])


def _moe(x1, gates, w_gate, w_up, w_down, ln_g, ln_b, layer, tm, alpha):
    n = x1.shape[0]
    dff = w_gate.shape[-1]
    vec = pl.BlockSpec((1, 1, D_MODEL), lambda i, e: (layer, 0, 0))
    return pl.pallas_call(
        functools.partial(_moe_body, alpha=alpha),
        grid=(n // tm, N_EXPERTS),
        in_specs=[
            pl.BlockSpec((tm, D_MODEL), lambda i, e: (i, 0)),
            pl.BlockSpec((tm, LANES), lambda i, e: (i, 0)),
            pl.BlockSpec((1, 1, D_MODEL, dff), lambda i, e: (layer, e, 0, 0)),
            pl.BlockSpec((1, 1, D_MODEL, dff), lambda i, e: (layer, e, 0, 0)),
            pl.BlockSpec((1, 1, dff, D_MODEL), lambda i, e: (layer, e, 0, 0)),
            vec, vec,
        ],
        out_specs=pl.BlockSpec((tm, D_MODEL), lambda i, e: (i, 0)),
        out_shape=jax.ShapeDtypeStruct((n, D_MODEL), F32),
        scratch_shapes=[pltpu.VMEM((tm, D_MODEL), BF16)],
        compiler_params=_cparams(("arbitrary", "arbitrary")),
        name="moe",
    )(x1, gates, w_gate, w_up, w_down, ln_g, ln_b)


def _rope_tables(pos):
    half = RET_D // 2
    inv = ROPE_BASE ** (-jnp.arange(half, dtype=F32) / half)
    ang = pos.astype(F32)[:, None] * inv[None, :]
    cos, sin = jnp.cos(ang), jnp.sin(ang)
    return jnp.concatenate([cos, cos], axis=1), jnp.concatenate([-sin, sin], axis=1)


def kernel(x_prompt, x_sample, cache_k, cache_v, cache_kidx, state_ret, page_table, w_in, ret_gn_w, w_out,
           ln1_g, ln1_b, router_w, router_b, w_gate, w_up, w_down, ln2_g, ln2_b):
    depth = w_in.shape[0]
    B, T, D = x_prompt.shape
    DB, TS, _ = x_sample.shape
    n_pool = cache_k.shape[1]
    W = ATT_HEADS * ATT_HD
    past = page_table.shape[1] * PAGE
    alpha = (2 * depth) ** 0.25

    w_main = w_in[:, :, :MAIN_W].astype(BF16)
    w_tail = jnp.pad(w_in[:, :, MAIN_W:], ((0, 0), (0, 0), (0, TAIL_W - (w_in.shape[2] - MAIN_W)))).astype(BF16)
    cache_k_t = cache_k.transpose(0, 1, 3, 4, 2).reshape(depth, n_pool, W, PAGE)
    cache_v_t = cache_v.transpose(0, 1, 3, 4, 2).reshape(depth, n_pool, W, PAGE)
    cache_ki_t = cache_kidx.transpose(0, 1, 3, 2)
    rwt = router_w.T
    rb_col = router_b.reshape(N_EXPERTS, 1)
    vec3 = lambda a: a.reshape(depth, 1, a.shape[-1])
    gn3, l1g, l1b, l2g, l2b = vec3(ret_gn_w), vec3(ln1_g), vec3(ln1_b), vec3(ln2_g), vec3(ln2_b)
    cos_p, sin_p = _rope_tables(jnp.arange(T))
    cos_s, sin_s = _rope_tables(past + jnp.arange(TS))
    zero_state = jnp.zeros((B, RET_HEADS, RET_D, RET_D), F32)
    C = min(LANES, T)
    wg_b, wu_b, wd_b = w_gate.astype(BF16), w_up.astype(BF16), w_down.astype(BF16)

    def block(x2d, nb, nt, o_att_fn, cosf, sinf, state0, chunk, tm_proj, tm_merge, tm_moe, moe_fn, l):
        pm, pt = _project(x2d, w_main, w_tail, l, tm_proj)
        pm3 = pm.reshape(nb, nt, MAIN_W)
        pt3 = pt.reshape(nb, nt, TAIL_W)
        o_ret, st = _retention(pm3, cosf, sinf, state0, chunk)
        o_att = o_att_fn(pm3, pt3)
        x1, gates, route, counts = _merge(o_ret.reshape(nb * nt, -1), pm, o_att.reshape(nb * nt, -1), x2d, gn3,
                                          w_out, l1g, l1b, rwt, rb_col, l, tm_merge, alpha, tm_moe // tm_merge)
        routing = (route, counts) if moe_fn is _moe_routed else gates
        x2 = moe_fn(x1, routing, wg_b, wu_b, wd_b, l2g, l2b, l, tm_moe, alpha)
        k_a = pm3[:, :, 5 * W:6 * W].reshape(nb, nt, ATT_HEADS, ATT_HD)
        v_a = pm3[:, :, 6 * W:7 * W].reshape(nb, nt, ATT_HEADS, ATT_HD)
        k_i = pt3[:, :, :IDX_DIM]
        return x2, k_a, v_a, k_i, st

    xp = x_prompt.reshape(B * T, D)
    xs = x_sample.reshape(DB * TS, D)
    outs_p, outs_s = [], []
    for l in range(depth):
        xp, *rest = block(xp, B, T, _dsa_prompt, cos_p, sin_p, zero_state, C,
                          min(2048, B * T), min(512, B * T), min(2048, B * T), _moe_routed, l)
        outs_p.append(rest)
        sample_att = functools.partial(_dsa_sample, cache_k_t=cache_k_t, cache_v_t=cache_v_t,
                                       cache_ki_t=cache_ki_t, page_table=page_table, layer=l)
        xs, *rest = block(xs, DB, TS, sample_att, cos_s, sin_s, state_ret[l], TS,
                          DB * TS, DB * TS, DB * TS, _moe, l)
        outs_s.append(rest)
    stack = lambda outs, i: jnp.stack([o[i] for o in outs])
    return (xp.reshape(B, T, D), xs.reshape(DB, TS, D),
            stack(outs_p, 0), stack(outs_p, 1), stack(outs_p, 2), stack(outs_p, 3),
            stack(outs_s, 0), stack(outs_s, 1), stack(outs_s, 2), stack(outs_s, 3))
```

```python
import functools

import numpy as np
import jax
import jax.numpy as jnp
from jax import lax
from jax.experimental import pallas as pl
from jax.experimental.pallas import tpu as pltpu

F32 = jnp.float32
BF16 = jnp.bfloat16
I32 = jnp.int32

RET_HEADS = 4
RET_D = 128
ATT_HEADS = 8
ATT_HD = 64
IDX_HEADS = 8
IDX_DIM = 64
TOPK_MAX = 256
PAGE = 128
N_EXPERTS = 16
N_GROUPS = 4
GROUP_SIZE = N_EXPERTS // N_GROUPS
ROPE_BASE = 10000.0
LN_EPS = 1e-5
D_MODEL = 1024
MAIN_W = 4096
TAIL_W = 128
LANES = 128
VMEM_LIMIT = 56 * 1024 * 1024

INT_MIN = -(2 ** 31)
NEG_INF = float("-inf")


def _log_gammas():
    h = np.arange(RET_HEADS, dtype=np.float32)
    return [float(v) for v in np.log1p(-np.exp2(-5.0 - h)).astype(np.float32)]


def _cparams(sem):
    return pltpu.CompilerParams(dimension_semantics=sem, vmem_limit_bytes=VMEM_LIMIT)


def _proj_body(x_ref, wm_ref, wt_ref, pm_ref, pt_ref, xb_ref):
    @pl.when(pl.program_id(1) == 0)
    def _():
        xb = x_ref[...].astype(BF16)
        xb_ref[...] = xb
        pt_ref[...] = jnp.dot(xb, wt_ref[0], preferred_element_type=F32)

    pm_ref[...] = jnp.dot(xb_ref[...], wm_ref[0], preferred_element_type=F32)


def _project(x2d, w_main, w_tail, layer, tm):
    n = x2d.shape[0]
    tn = 512
    return pl.pallas_call(
        _proj_body,
        grid=(n // tm, MAIN_W // tn),
        in_specs=[
            pl.BlockSpec((tm, D_MODEL), lambda i, j: (i, 0)),
            pl.BlockSpec((1, D_MODEL, tn), lambda i, j: (layer, 0, j)),
            pl.BlockSpec((1, D_MODEL, TAIL_W), lambda i, j: (layer, 0, 0)),
        ],
        out_specs=[
            pl.BlockSpec((tm, tn), lambda i, j: (i, j)),
            pl.BlockSpec((tm, TAIL_W), lambda i, j: (i, 0)),
        ],
        out_shape=[jax.ShapeDtypeStruct((n, MAIN_W), F32), jax.ShapeDtypeStruct((n, TAIL_W), F32)],
        scratch_shapes=[pltpu.VMEM((tm, D_MODEL), BF16)],
        compiler_params=_cparams(("arbitrary", "arbitrary")),
        name="proj",
    )(x2d, w_main, w_tail)


def _ret_body(q_ref, k_ref, v_ref, cos_ref, sin_ref, s0_ref, o_ref, sout_ref, st_ref, *, C, NC):
    c = pl.program_id(1)
    CP = max(C, LANES)

    @pl.when(c == 0)
    def _():
        st_ref[...] = s0_ref[0]

    def pad(a):
        if CP == C:
            return a
        return jnp.concatenate([a, jnp.zeros((CP - C, a.shape[1]), a.dtype)], axis=0)

    cosf = pad(cos_ref[...])
    sinf = pad(sin_ref[...])
    ii = lax.broadcasted_iota(I32, (CP, CP), 0)
    jj = lax.broadcasted_iota(I32, (CP, CP), 1)
    diff = (ii - jj).astype(F32)
    pos = lax.broadcasted_iota(I32, (CP, 1), 0).astype(F32)
    for h, lg in enumerate(_log_gammas()):
        sl = slice(h * RET_D, (h + 1) * RET_D)
        q = pad(q_ref[0, :, sl])
        k = pad(k_ref[0, :, sl])
        vb = pad(v_ref[0, :, sl]).astype(BF16)
        q = q * cosf + pltpu.roll(q, RET_D // 2, 1) * sinf
        k = (k * cosf + pltpu.roll(k, RET_D // 2, 1) * sinf) * (RET_D ** -0.5)
        decay = jnp.where(diff >= 0, jnp.exp(lg * jnp.maximum(diff, 0.0)), 0.0)
        qb = q.astype(BF16)
        scores = lax.dot_general(qb, k.astype(BF16), (((1,), (1,)), ((), ())),
                                 preferred_element_type=F32) * decay
        inner = jnp.dot(scores.astype(BF16), vb, preferred_element_type=F32)
        st = st_ref[h]
        cross = jnp.dot(qb, st.astype(BF16), preferred_element_type=F32) * jnp.exp(lg * (pos + 1.0))
        o_ref[0, :, sl] = (inner + cross)[:C]
        kd = k * jnp.exp(lg * (C - 1.0 - pos))
        st_ref[h] = st * float(np.exp(np.float32(lg) * np.float32(C))) + jnp.dot(
            kd.T.astype(BF16), vb, preferred_element_type=F32)

    @pl.when(c == NC - 1)
    def _():
        sout_ref[0] = st_ref[...]


def _retention(pm3, cosf, sinf, state0, C):
    B, T, _ = pm3.shape
    NC = T // C
    W = RET_HEADS * RET_D
    return pl.pallas_call(
        functools.partial(_ret_body, C=C, NC=NC),
        grid=(B, NC),
        in_specs=[
            pl.BlockSpec((1, C, W), lambda b, c: (b, c, 0)),
            pl.BlockSpec((1, C, W), lambda b, c: (b, c, 1)),
            pl.BlockSpec((1, C, W), lambda b, c: (b, c, 2)),
            pl.BlockSpec((C, RET_D), lambda b, c: (c, 0)),
            pl.BlockSpec((C, RET_D), lambda b, c: (c, 0)),
            pl.BlockSpec((1, RET_HEADS, RET_D, RET_D), lambda b, c: (b, 0, 0, 0)),
        ],
        out_specs=[
            pl.BlockSpec((1, C, W), lambda b, c: (b, c, 0)),
            pl.BlockSpec((1, RET_HEADS, RET_D, RET_D), lambda b, c: (b, 0, 0, 0)),
        ],
        out_shape=[jax.ShapeDtypeStruct((B, T, W), F32),
                   jax.ShapeDtypeStruct((B, RET_HEADS, RET_D, RET_D), F32)],
        scratch_shapes=[pltpu.VMEM((RET_HEADS, RET_D, RET_D), F32)],
        compiler_params=_cparams(("arbitrary", "arbitrary")),
        name="retention",
    )(pm3, pm3, pm3, cosf, sinf, state0)


def _sortable(x):
    bits = pltpu.bitcast(x + 0.0, I32)
    return bits ^ ((bits >> 31) & 0x7FFFFFFF)


def _kth_largest(count_ge, shape, kf):
    c0 = count_ge(jnp.zeros(shape, I32))
    tau = jnp.where(c0 >= kf, 0, INT_MIN).astype(I32)

    def body(it, tau):
        cand = tau | (jnp.int32(1) << (30 - it))
        return jnp.where(count_ge(cand) >= kf, cand, tau)

    return lax.fori_loop(0, 31, body, tau)


def _tie_cutoff(count_lt, shape, need, nbits):
    def body(it, m):
        cand = m + (jnp.int32(1) << (nbits - 1 - it))
        return jnp.where(count_lt(cand) < need, cand, m)

    return lax.fori_loop(0, nbits, body, jnp.zeros(shape, I32))


def _col_reduce(x, op):
    S, Q = x.shape
    if S > LANES:
        x = op(x.reshape(S // LANES, LANES, Q), axis=0)
    return op(x, axis=0, keepdims=True)


def _dsa_prompt_block(i, S, qa_ref, qi_ref, ptq_ref, o_ref, kb_ref, vt_ref, kib_ref, keys_ref, bias_ref,
                      *, QB, ksel):
    RC = 256 if S % 256 == 0 else LANES
    qiT = qi_ref[0].T
    ptT = ptq_ref[0].T
    zeros64 = jnp.zeros((IDX_DIM, QB), BF16)
    rhs_idx = jnp.concatenate(
        [jnp.concatenate([qiT[h * IDX_DIM:(h + 1) * IDX_DIM].astype(BF16), zeros64], axis=0)
         for h in range(IDX_HEADS)], axis=1)
    w_rows = [ptT[IDX_DIM + h:IDX_DIM + h + 1, :] * (IDX_HEADS ** -0.5) * (IDX_DIM ** -0.5)
              for h in range(IDX_HEADS)]
    t_pos = i * QB + lax.broadcasted_iota(I32, (RC, QB), 1)
    for r in range(S // RC):
        d = jnp.dot(kib_ref[r * RC:(r + 1) * RC, :], rhs_idx, preferred_element_type=F32)
        acc = jnp.maximum(d[:, 0:QB], 0.0) * w_rows[0]
        for h in range(1, IDX_HEADS):
            acc = acc + jnp.maximum(d[:, h * QB:(h + 1) * QB], 0.0) * w_rows[h]
        s_pos = r * RC + lax.broadcasted_iota(I32, (RC, QB), 0)
        keys_ref[r * RC:(r + 1) * RC, :] = _sortable(jnp.where(s_pos <= t_pos, acc, NEG_INF))

    kf = float(ksel)
    NT = S // LANES

    def count(pred):
        acc = jnp.zeros((LANES, QB), F32)
        for r in range(NT):
            acc = jnp.where(pred(keys_ref[r * LANES:(r + 1) * LANES, :], r * LANES), acc + 1.0, acc)
        return jnp.sum(acc, axis=0, keepdims=True)

    tau = _kth_largest(lambda cand: count(lambda k, _: k >= cand), (1, QB), kf)
    row = lax.broadcasted_iota(I32, (LANES, QB), 0)
    t_row = i * QB + lax.broadcasted_iota(I32, (LANES, QB), 1)
    need = kf - count(lambda k, _: k > tau)
    n_eq = count(lambda k, r0: (k == tau) & (r0 + row <= t_row))
    any_split = jnp.max(jnp.where(n_eq > need, 1.0, 0.0)) > 0.0

    def slow():
        return _tie_cutoff(lambda m: count(lambda k, r0: (k == tau) & (r0 + row <= t_row) & (r0 + row < m)),
                           (1, QB), need, int(S).bit_length())

    mcut = lax.cond(any_split, slow, lambda: jnp.full((1, QB), S, I32))
    for r in range(NT):
        k = keys_ref[r * LANES:(r + 1) * LANES, :]
        s_pos = r * LANES + row
        sel = (s_pos <= t_row) & ((k > tau) | ((k == tau) & (s_pos <= mcut)))
        bias_ref[r * LANES:(r + 1) * LANES, :] = jnp.where(sel, 0.0, NEG_INF)

    qaT = qa_ref[0].T * (ATT_HD ** -0.5)
    zq = jnp.zeros((ATT_HD, QB), BF16)
    outs = []
    for j in range(ATT_HEADS // 2):
        r0 = 2 * j * ATT_HD
        top = jnp.concatenate([qaT[r0:r0 + ATT_HD].astype(BF16), zq], axis=1)
        bot = jnp.concatenate([zq, qaT[r0 + ATT_HD:r0 + 2 * ATT_HD].astype(BF16)], axis=1)
        rhs = jnp.concatenate([top, bot], axis=0)
        sT = jnp.dot(kb_ref[0:S, r0:r0 + 2 * ATT_HD], rhs, preferred_element_type=F32)
        for u in range(2):
            h = 2 * j + u
            s = sT[:, u * QB:(u + 1) * QB] + bias_ref[0:S, :]
            m = _col_reduce(s, jnp.max)
            p = jnp.exp(s - m)
            l = _col_reduce(p, jnp.sum)
            oT = jnp.dot(vt_ref[h * ATT_HD:(h + 1) * ATT_HD, 0:S], p.astype(BF16),
                         preferred_element_type=F32)
            outs.append(oT / l)
    o_ref[0] = jnp.concatenate(outs, axis=0).T


def _dsa_prompt_body(qa_ref, k_ref, v_ref, qi_ref, pt_ref, ptq_ref, o_ref,
                     kb_ref, vt_ref, kib_ref, keys_ref, bias_ref, *, T, QB, ksel, NV):
    i = pl.program_id(1)

    @pl.when(i == 0)
    def _():
        kb_ref[...] = k_ref[0].astype(BF16)
        vt_ref[...] = v_ref[0].T.astype(BF16)
        kib_ref[...] = pt_ref[0].astype(BF16)

    per = (T // QB) // NV
    for g in range(NV):
        @pl.when(i // per == g)
        def _(g=g):
            _dsa_prompt_block(i, (g + 1) * per * QB, qa_ref, qi_ref, ptq_ref, o_ref, kb_ref, vt_ref, kib_ref,
                              keys_ref, bias_ref, QB=QB, ksel=ksel)


def _dsa_prompt(pm3, pt3):
    B, T, _ = pm3.shape
    QB = min(LANES, T)
    ksel = min(TOPK_MAX, T // 4)
    W = ATT_HEADS * ATT_HD
    NV = min(8, T // QB)
    return pl.pallas_call(
        functools.partial(_dsa_prompt_body, T=T, QB=QB, ksel=ksel, NV=NV),
        grid=(B, T // QB),
        in_specs=[
            pl.BlockSpec((1, QB, W), lambda b, i: (b, i, 4)),
            pl.BlockSpec((1, T, W), lambda b, i: (b, 0, 5)),
            pl.BlockSpec((1, T, W), lambda b, i: (b, 0, 6)),
            pl.BlockSpec((1, QB, W), lambda b, i: (b, i, 7)),
            pl.BlockSpec((1, T, TAIL_W), lambda b, i: (b, 0, 0)),
            pl.BlockSpec((1, QB, TAIL_W), lambda b, i: (b, i, 0)),
        ],
        out_specs=pl.BlockSpec((1, QB, W), lambda b, i: (b, i, 0)),
        out_shape=jax.ShapeDtypeStruct((B, T, W), F32),
        scratch_shapes=[pltpu.VMEM((T, W), BF16), pltpu.VMEM((W, T), BF16),
                        pltpu.VMEM((T, TAIL_W), BF16), pltpu.VMEM((T, QB), I32),
                        pltpu.VMEM((T, QB), F32)],
        compiler_params=_cparams(("arbitrary", "arbitrary")),
        name="dsa_prompt",
    )(pm3, pm3, pm3, pm3, pt3, pt3)


PAGES_PER_STEP = 16
IDX_PAGES_PER_STEP = 64


def _idx_sample_body(ptab_ref, *refs, P):
    del ptab_ref
    page_refs = refs[:P]
    qi_ref, w_ref, kin_ref, sc_ref, scn_ref = refs[P:]
    q = qi_ref[0].astype(BF16)
    w = w_ref[0] * (IDX_HEADS ** -0.5) * (IDX_DIM ** -0.5)
    nq = q.shape[0] // IDX_HEADS

    def scores(keys_t):
        d = jnp.dot(q, keys_t.astype(BF16), preferred_element_type=F32)
        r = jnp.maximum(d, 0.0) * w
        acc = r[0:nq]
        for h in range(1, IDX_HEADS):
            acc = acc + r[h * nq:(h + 1) * nq]
        return acc

    sc_ref[0] = scores(jnp.concatenate([r[0, 0] for r in page_refs], axis=1))

    @pl.when(pl.program_id(1) == 0)
    def _():
        scn_ref[0] = scores(kin_ref[0])


def _idx_sample(page_table, cache_ki_t, layer, qi_flat, w_col, ki_new_t):
    DB, n_pages = page_table.shape
    P = int(np.gcd(n_pages, IDX_PAGES_PER_STEP))
    NCH = n_pages // P
    nq = qi_flat.shape[1] // IDX_HEADS

    def page_spec(p):
        return pl.BlockSpec((1, 1, IDX_DIM, PAGE), lambda b, c, pt: (layer, pt[b, c * P + p], 0, 0))

    grid_spec = pltpu.PrefetchScalarGridSpec(
        num_scalar_prefetch=1,
        grid=(DB, NCH),
        in_specs=[page_spec(p) for p in range(P)] + [
            pl.BlockSpec((1, IDX_HEADS * nq, IDX_DIM), lambda b, c, pt: (b, 0, 0)),
            pl.BlockSpec((1, IDX_HEADS * nq, 1), lambda b, c, pt: (b, 0, 0)),
            pl.BlockSpec((1, IDX_DIM, PAGE), lambda b, c, pt: (b, 0, 0)),
        ],
        out_specs=[
            pl.BlockSpec((1, nq, P * PAGE), lambda b, c, pt: (b, 0, c)),
            pl.BlockSpec((1, nq, PAGE), lambda b, c, pt: (b, 0, 0)),
        ],
    )
    return pl.pallas_call(
        functools.partial(_idx_sample_body, P=P),
        grid_spec=grid_spec,
        out_shape=[jax.ShapeDtypeStruct((DB, nq, NCH * P * PAGE), F32),
                   jax.ShapeDtypeStruct((DB, nq, PAGE), F32)],
        compiler_params=_cparams(("arbitrary", "arbitrary")),
        name="idx_sample",
    )(page_table, *([cache_ki_t] * P), qi_flat, w_col, ki_new_t)


def _select_sample_body(sc_ref, scn_ref, bm_ref, bn_ref, keys_ref, *, GB, nq, past, ksel):
    R = GB * nq
    S = past + PAGE
    NT = S // LANES
    t_of_row = lax.broadcasted_iota(I32, (GB, nq, LANES), 1).reshape(R, LANES)
    lane = lax.broadcasted_iota(I32, (R, LANES), 1)
    new_valid = lane <= t_of_row
    keys_ref[:, :past] = _sortable(sc_ref[...].reshape(R, past))
    keys_ref[:, past:] = _sortable(jnp.where(new_valid, scn_ref[...].reshape(R, PAGE), NEG_INF))
    kf = float(ksel)

    def count(pred):
        acc = jnp.zeros((R, LANES), F32)
        for c in range(NT):
            ok = pred(keys_ref[:, c * LANES:(c + 1) * LANES], c * LANES, new_valid if c == NT - 1 else None)
            acc = jnp.where(ok, acc + 1.0, acc)
        return jnp.sum(acc, axis=1, keepdims=True)

    def tied(k, valid, tau):
        return (k == tau) if valid is None else (k == tau) & valid

    tau = _kth_largest(lambda cand: count(lambda k, c0, v: k >= cand), (R, 1), kf)
    need = kf - count(lambda k, c0, v: k > tau)
    n_eq = count(lambda k, c0, v: tied(k, v, tau))
    any_split = jnp.max(jnp.where(n_eq > need, 1.0, 0.0)) > 0.0

    def slow():
        return _tie_cutoff(lambda m: count(lambda k, c0, v: tied(k, v, tau) & (c0 + lane < m)),
                           (R, 1), need, int(S).bit_length())

    mcut = lax.cond(any_split, slow, lambda: jnp.full((R, 1), S, I32))
    for c in range(NT):
        k = keys_ref[:, c * LANES:(c + 1) * LANES]
        valid = new_valid if c == NT - 1 else None
        sel = (k > tau) | (tied(k, valid, tau) & (c * LANES + lane <= mcut))
        if valid is not None:
            sel = sel & valid
        bias = jnp.where(sel, 0.0, NEG_INF).reshape(GB, nq, LANES)
        if c == NT - 1:
            bn_ref[...] = bias
        else:
            bm_ref[:, :, c * LANES:(c + 1) * LANES] = bias


def _select_sample(sc, scn, ksel):
    DB, nq, past = sc.shape
    GB = int(np.gcd(DB, 8))
    return pl.pallas_call(
        functools.partial(_select_sample_body, GB=GB, nq=nq, past=past, ksel=ksel),
        grid=(DB // GB,),
        in_specs=[pl.BlockSpec((GB, nq, past), lambda b: (b, 0, 0)),
                  pl.BlockSpec((GB, nq, PAGE), lambda b: (b, 0, 0))],
        out_specs=[pl.BlockSpec((GB, nq, past), lambda b: (b, 0, 0)),
                   pl.BlockSpec((GB, nq, PAGE), lambda b: (b, 0, 0))],
        out_shape=[jax.ShapeDtypeStruct((DB, nq, past), F32),
                   jax.ShapeDtypeStruct((DB, nq, PAGE), F32)],
        scratch_shapes=[pltpu.VMEM((GB * nq, past + PAGE), I32)],
        compiler_params=_cparams(("arbitrary",)),
        name="select_sample",
    )(sc, scn)


def _att_sample_body(ptab_ref, *refs, P, NCH, nq):
    del ptab_ref
    k_refs = refs[:P]
    v_refs = refs[P:2 * P]
    q_ref, bm_ref, bn_ref, kn_ref, vn_ref, o_ref, m_ref, l_ref, acc_ref = refs[2 * P:]
    c = pl.program_id(1)

    @pl.when(c == 0)
    def _():
        m_ref[...] = jnp.full(m_ref.shape, NEG_INF, F32)
        l_ref[...] = jnp.zeros(l_ref.shape, F32)
        acc_ref[...] = jnp.zeros(acc_ref.shape, F32)

    q = (q_ref[0] * (ATT_HD ** -0.5)).astype(BF16)

    def update(k_t, v_t, bias):
        s = jnp.dot(q, k_t.astype(BF16), preferred_element_type=F32)
        s = s + jnp.concatenate([bias] * ATT_HEADS, axis=0)
        m_old = m_ref[...]
        m_new = jnp.maximum(m_old, jnp.max(s, axis=1, keepdims=True))
        m_safe = jnp.where(m_new == NEG_INF, 0.0, m_new)
        alpha = jnp.exp(m_old - m_safe)
        p = jnp.exp(s - m_safe)
        l_ref[...] = l_ref[...] * alpha + jnp.sum(p, axis=1, keepdims=True)
        acc_ref[...] = acc_ref[...] * alpha + lax.dot_general(
            p.astype(BF16), v_t.astype(BF16), (((1,), (1,)), ((), ())), preferred_element_type=F32)
        m_ref[...] = m_new

    update(jnp.concatenate([r[0, 0] for r in k_refs], axis=1),
           jnp.concatenate([r[0, 0] for r in v_refs], axis=1), bm_ref[0])

    @pl.when(c == NCH - 1)
    def _():
        update(kn_ref[0], vn_ref[0], bn_ref[0])
        o = acc_ref[...] / l_ref[...]
        head_of_lane = lax.broadcasted_iota(I32, (nq, ATT_HEADS * ATT_HD), 1) // ATT_HD
        out = jnp.zeros((nq, ATT_HEADS * ATT_HD), F32)
        for h in range(ATT_HEADS):
            out = out + jnp.where(head_of_lane == h, o[h * nq:(h + 1) * nq], 0.0)
        o_ref[0] = out


def _att_sample(page_table, cache_k_t, cache_v_t, layer, q_bd, bias_main, bias_new, k_new_t, v_new_t):
    DB, n_pages = page_table.shape
    P = int(np.gcd(n_pages, PAGES_PER_STEP))
    NCH = n_pages // P
    nq = bias_main.shape[1]
    W = ATT_HEADS * ATT_HD

    def page_spec(p):
        return pl.BlockSpec((1, 1, W, PAGE), lambda b, c, pt: (layer, pt[b, c * P + p], 0, 0))

    grid_spec = pltpu.PrefetchScalarGridSpec(
        num_scalar_prefetch=1,
        grid=(DB, NCH),
        in_specs=[page_spec(p) for p in range(P)] + [page_spec(p) for p in range(P)] + [
            pl.BlockSpec((1, ATT_HEADS * nq, W), lambda b, c, pt: (b, 0, 0)),
            pl.BlockSpec((1, nq, P * PAGE), lambda b, c, pt: (b, 0, c)),
            pl.BlockSpec((1, nq, PAGE), lambda b, c, pt: (b, 0, 0)),
            pl.BlockSpec((1, W, PAGE), lambda b, c, pt: (b, 0, 0)),
            pl.BlockSpec((1, W, PAGE), lambda b, c, pt: (b, 0, 0)),
        ],
        out_specs=pl.BlockSpec((1, nq, W), lambda b, c, pt: (b, 0, 0)),
        scratch_shapes=[pltpu.VMEM((ATT_HEADS * nq, 1), F32), pltpu.VMEM((ATT_HEADS * nq, 1), F32),
                        pltpu.VMEM((ATT_HEADS * nq, W), F32)],
    )
    return pl.pallas_call(
        functools.partial(_att_sample_body, P=P, NCH=NCH, nq=nq),
        grid_spec=grid_spec,
        out_shape=jax.ShapeDtypeStruct((DB, nq, W), F32),
        compiler_params=_cparams(("arbitrary", "arbitrary")),
        name="att_sample",
    )(page_table, *([cache_k_t] * P), *([cache_v_t] * P), q_bd, bias_main, bias_new, k_new_t, v_new_t)


def _dsa_sample(pm3, pt3, cache_k_t, cache_v_t, cache_ki_t, page_table, layer):
    DB, nq, _ = pm3.shape
    W = ATT_HEADS * ATT_HD
    past = page_table.shape[1] * PAGE
    ksel = min(TOPK_MAX, (past + nq) // 4)
    q_a = pm3[:, :, 4 * W:5 * W].reshape(DB, nq, ATT_HEADS, ATT_HD)
    q_i = pm3[:, :, 7 * W:8 * W].reshape(DB, nq, IDX_HEADS, IDX_DIM)
    qi_flat = q_i.transpose(0, 2, 1, 3).reshape(DB, IDX_HEADS * nq, IDX_DIM)
    w_col = pt3[:, :, IDX_DIM:IDX_DIM + IDX_HEADS].transpose(0, 2, 1).reshape(DB, IDX_HEADS * nq, 1)
    q_bd = jnp.einsum("bthd,hg->bhtgd", q_a, jnp.eye(ATT_HEADS, dtype=F32)).reshape(DB, ATT_HEADS * nq, W)

    def new_t(a):
        return jnp.pad(a.transpose(0, 2, 1), ((0, 0), (0, 0), (0, PAGE - nq)))

    sc, scn = _idx_sample(page_table, cache_ki_t, layer, qi_flat, w_col, new_t(pt3[:, :, :IDX_DIM]))
    bias_main, bias_new = _select_sample(sc, scn, ksel)
    return _att_sample(page_table, cache_k_t, cache_v_t, layer, q_bd, bias_main, bias_new,
                       new_t(pm3[:, :, 5 * W:6 * W]), new_t(pm3[:, :, 6 * W:7 * W]))


def _layer_norm(y, g, b):
    mu = jnp.mean(y, axis=-1, keepdims=True)
    var = jnp.mean(jnp.square(y - mu), axis=-1, keepdims=True)
    return (y - mu) * lax.rsqrt(var + LN_EPS) * g + b


def _top2_of4(b0, b1, b2, b3):
    hi01, lo01 = jnp.maximum(b0, b1), jnp.minimum(b0, b1)
    hi23, lo23 = jnp.maximum(b2, b3), jnp.minimum(b2, b3)
    return jnp.maximum(hi01, hi23), jnp.maximum(jnp.minimum(hi01, hi23), jnp.maximum(lo01, lo23))


def _route_rows(aff, biased):
    gscore = []
    for g in range(N_GROUPS):
        m1, m2 = _top2_of4(*biased[g * GROUP_SIZE:(g + 1) * GROUP_SIZE])
        gscore.append(m1 + m2)
    gmax = functools.reduce(jnp.maximum, gscore)
    taken = jnp.zeros_like(gmax, dtype=jnp.bool_)
    gsel = []
    for g in range(N_GROUPS):
        hit = (gscore[g] == gmax) & jnp.logical_not(taken)
        gsel.append(hit)
        taken = taken | hit
    cand, caff = [], []
    for k in range(GROUP_SIZE):
        c = jnp.full_like(gmax, NEG_INF)
        a = jnp.zeros_like(gmax)
        for g in range(N_GROUPS):
            c = jnp.where(gsel[g], biased[g * GROUP_SIZE + k], c)
            a = jnp.where(gsel[g], aff[g * GROUP_SIZE + k], a)
        cand.append(c)
        caff.append(a)
    best = functools.reduce(jnp.maximum, cand)
    taken = jnp.zeros_like(taken)
    first = []
    for k in range(GROUP_SIZE):
        hit = (cand[k] == best) & jnp.logical_not(taken)
        first.append(hit)
        taken = taken | hit
    rest = [jnp.where(first[k], NEG_INF, cand[k]) for k in range(GROUP_SIZE)]
    best2 = functools.reduce(jnp.maximum, rest)
    taken = jnp.zeros_like(taken)
    second = []
    for k in range(GROUP_SIZE):
        hit = (rest[k] == best2) & jnp.logical_not(first[k]) & jnp.logical_not(taken)
        second.append(hit)
        taken = taken | hit
    a1 = functools.reduce(jnp.add, [jnp.where(first[k], caff[k], 0.0) for k in range(GROUP_SIZE)])
    a2 = functools.reduce(jnp.add, [jnp.where(second[k], caff[k], 0.0) for k in range(GROUP_SIZE)])
    tot = a1 + a2
    w1, w2 = a1 / tot, a2 / tot
    gates, chosen = [], []
    for g in range(N_GROUPS):
        for k in range(GROUP_SIZE):
            gates.append(jnp.where(gsel[g] & first[k], w1, jnp.where(gsel[g] & second[k], w2, 0.0)))
            chosen.append(jnp.where(gsel[g] & (first[k] | second[k]), 1.0, 0.0))
    return gates, chosen


def _merge_body(oret_ref, g_ref, oatt_ref, x_ref, gn_ref, wout_ref, lg_ref, lb_ref, rwt_ref, rb_ref,
                x1_ref, gates_ref, route_ref, cnt_ref, wob_ref, carry_ref, *, alpha, group):
    @pl.when(pl.program_id(0) == 0)
    def _():
        wob_ref[...] = wout_ref[0].astype(BF16)

    @pl.when(pl.program_id(0) % group == 0)
    def _():
        carry_ref[...] = jnp.zeros(carry_ref.shape, F32)

    parts = []
    for h in range(RET_HEADS):
        sl = slice(h * RET_D, (h + 1) * RET_D)
        o = oret_ref[:, sl]
        mu = jnp.mean(o, axis=-1, keepdims=True)
        var = jnp.mean(jnp.square(o - mu), axis=-1, keepdims=True)
        r = (o - mu) * lax.rsqrt(var + LN_EPS) * gn_ref[0, :, sl]
        g = g_ref[:, sl]
        parts.append(g * jax.nn.sigmoid(g) * r)
    cat = jnp.concatenate(parts + [oatt_ref[...]], axis=1).astype(BF16)
    mix = jnp.dot(cat, wob_ref[...], preferred_element_type=F32)
    x1 = _layer_norm(alpha * x_ref[...] + mix, lg_ref[0], lb_ref[0])
    x1_ref[...] = x1

    logits = lax.dot_general(rwt_ref[...], x1, (((1,), (1,)), ((), ())),
                             precision=lax.Precision.HIGHEST, preferred_element_type=F32)
    aff = jax.nn.sigmoid(logits)
    biased = aff + rb_ref[...]
    gates, chosen = _route_rows([aff[e:e + 1] for e in range(N_EXPERTS)],
                                [biased[e:e + 1] for e in range(N_EXPERTS)])
    tm = x1.shape[0]
    gt = jnp.concatenate(gates + chosen + [jnp.zeros((LANES - 2 * N_EXPERTS, tm), F32)], axis=0)
    gates_ref[...] = gt.T

    cmat = jnp.concatenate(chosen, axis=0)
    upper = (lax.broadcasted_iota(I32, (tm, tm), 0) <= lax.broadcasted_iota(I32, (tm, tm), 1))
    csum = jnp.dot(cmat.astype(BF16), jnp.where(upper, 1.0, 0.0).astype(BF16),
                   preferred_element_type=F32)
    carry = carry_ref[...]
    rank = carry[:, 0:1] + csum - 1.0
    carry = carry + csum[:, tm - 1:tm]
    carry_ref[...] = carry
    cnt_ref[0] = carry
    zero = jnp.zeros((1, tm), F32)
    seen, e1, e2, r1, r2, g1, g2 = zero, zero, zero, zero, zero, zero, zero
    for e in range(N_EXPERTS):
        hit = chosen[e] > 0.5
        first = hit & (seen == 0.0)
        second = hit & (seen == 1.0)
        e1 = jnp.where(first, float(e), e1)
        e2 = jnp.where(second, float(e), e2)
        r1 = jnp.where(first, rank[e:e + 1], r1)
        r2 = jnp.where(second, rank[e:e + 1], r2)
        g1 = jnp.where(first, gates[e], g1)
        g2 = jnp.where(second, gates[e], g2)
        seen = seen + chosen[e]
    route_ref[...] = jnp.concatenate([e1, e2, r1, r2, g1, g2, zero, zero], axis=0)


def _merge(o_ret, pm, o_att, x2d, gn_w, w_out, ln_g, ln_b, rwt, rb_col, layer, tm, alpha, group):
    n = x2d.shape[0]
    W = RET_HEADS * RET_D
    vec = lambda width: pl.BlockSpec((1, 1, width), lambda i: (layer, 0, 0))
    return pl.pallas_call(
        functools.partial(_merge_body, alpha=alpha, group=group),
        grid=(n // tm,),
        in_specs=[
            pl.BlockSpec((tm, W), lambda i: (i, 0)),
            pl.BlockSpec((tm, W), lambda i: (i, 3)),
            pl.BlockSpec((tm, W), lambda i: (i, 0)),
            pl.BlockSpec((tm, D_MODEL), lambda i: (i, 0)),
            vec(W),
            pl.BlockSpec((1, D_MODEL, D_MODEL), lambda i: (layer, 0, 0)),
            vec(D_MODEL), vec(D_MODEL),
            pl.BlockSpec((N_EXPERTS, D_MODEL), lambda i: (0, 0)),
            pl.BlockSpec((N_EXPERTS, 1), lambda i: (0, 0)),
        ],
        out_specs=[pl.BlockSpec((tm, D_MODEL), lambda i: (i, 0)),
                   pl.BlockSpec((tm, LANES), lambda i: (i, 0)),
                   pl.BlockSpec((8, tm), lambda i: (0, i)),
                   pl.BlockSpec((1, N_EXPERTS, LANES), lambda i: (i, 0, 0))],
        out_shape=[jax.ShapeDtypeStruct((n, D_MODEL), F32), jax.ShapeDtypeStruct((n, LANES), F32),
                   jax.ShapeDtypeStruct((8, n), F32),
                   jax.ShapeDtypeStruct((n // tm, N_EXPERTS, LANES), F32)],
        scratch_shapes=[pltpu.VMEM((D_MODEL, D_MODEL), BF16), pltpu.VMEM((N_EXPERTS, LANES), F32)],
        compiler_params=_cparams(("arbitrary",)),
        name="merge",
    )(o_ret, pm, o_att, x2d, gn_w, w_out, ln_g, ln_b, rwt, rb_col)


def _expert(xb, wg, wu, wd):
    a = jnp.dot(xb, wg, preferred_element_type=F32)
    u = jnp.dot(xb, wu, preferred_element_type=F32)
    hid = (a * jax.nn.sigmoid(a) * u).astype(BF16)
    return jnp.dot(hid, wd, preferred_element_type=F32)


def _moe_routed_body(p1_ref, p2_ref, g1_ref, g2_ref, off_ref, cnt_ref,
                     x_ref, wg_ref, wu_ref, wd_ref, lg_ref, lb_ref, out_ref, s_ref, *, TM, R, alpha):
    i = pl.program_id(0)
    e = pl.program_id(1)
    t0 = i * TM

    @pl.when(e == 0)
    def _():
        zeros8 = jnp.zeros((8, D_MODEL), F32)
        for ee in range(N_EXPERTS):
            n = cnt_ref[i * N_EXPERTS + ee]
            s_ref[pl.ds(pl.multiple_of(off_ref[i * N_EXPERTS + ee] + (n // 8) * 8, 8), 8), :] = zeros8
        last = i * N_EXPERTS + N_EXPERTS - 1
        end = off_ref[last] + ((cnt_ref[last] + 7) // 8) * 8
        s_ref[pl.ds(pl.multiple_of(end, 8), R), :] = jnp.zeros((R, D_MODEL), F32)

        def scatter(t, c):
            row = x_ref[pl.ds(t, 1), :]
            s_ref[pl.ds(p1_ref[t0 + t], 1), :] = row
            s_ref[pl.ds(p2_ref[t0 + t], 1), :] = row
            return c

        lax.fori_loop(0, TM, scatter, 0, unroll=8)

    n = cnt_ref[i * N_EXPERTS + e]
    start = off_ref[i * N_EXPERTS + e]
    wg, wu, wd = wg_ref[0, 0], wu_ref[0, 0], wd_ref[0, 0]

    def chunk(c, carry):
        st = pl.multiple_of(start + c * R, 8)
        xs = s_ref[pl.ds(st, R), :]
        y = _expert(xs.astype(BF16), wg, wu, wd)
        mine = c * R + lax.broadcasted_iota(I32, (R, 1), 0) < n
        s_ref[pl.ds(st, R), :] = jnp.where(mine, y, xs)
        return carry

    lax.fori_loop(0, (n + R - 1) // R, chunk, 0)

    @pl.when(e == N_EXPERTS - 1)
    def _():
        def gather(t, c):
            out_ref[pl.ds(t, 1), :] = (g1_ref[t0 + t] * s_ref[pl.ds(p1_ref[t0 + t], 1), :]
                                       + g2_ref[t0 + t] * s_ref[pl.ds(p2_ref[t0 + t], 1), :])
            return c

        lax.fori_loop(0, TM, gather, 0, unroll=8)
        out_ref[...] = _layer_norm(alpha * x_ref[...] + out_ref[...], lg_ref[0], lb_ref[0])


def _moe_routed(x1, routing, w_gate, w_up, w_down, ln_g, ln_b, layer, tm, alpha):
    route, counts = routing
    n = x1.shape[0]
    dff = w_gate.shape[-1]
    nt = n // tm
    R = min(288, tm)
    group = counts.shape[0] // nt
    cnt = counts[group - 1::group, :, 0].astype(I32)
    cnt8 = (cnt + 7) // 8 * 8
    off = jnp.cumsum(cnt8, axis=1) - cnt8
    expert_ids = jnp.arange(N_EXPERTS, dtype=I32)[:, None, None]

    def row_of(e_row, r_row):
        e = e_row.astype(I32).reshape(1, nt, tm)
        start = jnp.sum(jnp.where(e == expert_ids, off.T[:, :, None], 0), axis=0)
        return start + r_row.astype(I32).reshape(nt, tm)

    p1, p2 = row_of(route[0], route[2]), row_of(route[1], route[3])
    g1, g2 = route[4], route[5]
    rows = 2 * tm + 8 * N_EXPERTS + R
    vec = pl.BlockSpec((1, 1, D_MODEL), lambda i, e, *_: (layer, 0, 0))
    grid_spec = pltpu.PrefetchScalarGridSpec(
        num_scalar_prefetch=6,
        grid=(nt, N_EXPERTS),
        in_specs=[
            pl.BlockSpec((tm, D_MODEL), lambda i, e, *_: (i, 0), pipeline_mode=pl.Buffered(1)),
            pl.BlockSpec((1, 1, D_MODEL, dff), lambda i, e, *_: (layer, e, 0, 0)),
            pl.BlockSpec((1, 1, D_MODEL, dff), lambda i, e, *_: (layer, e, 0, 0)),
            pl.BlockSpec((1, 1, dff, D_MODEL), lambda i, e, *_: (layer, e, 0, 0)),
            vec, vec,
        ],
        out_specs=pl.BlockSpec((tm, D_MODEL), lambda i, e, *_: (i, 0), pipeline_mode=pl.Buffered(1)),
        scratch_shapes=[pltpu.VMEM((rows, D_MODEL), F32)],
    )
    return pl.pallas_call(
        functools.partial(_moe_routed_body, TM=tm, R=R, alpha=alpha),
        grid_spec=grid_spec,
        out_shape=jax.ShapeDtypeStruct((n, D_MODEL), F32),
        compiler_params=_cparams(("arbitrary", "arbitrary")),
        name="moe_routed",
    )(p1.reshape(-1), p2.reshape(-1), g1.reshape(-1), g2.reshape(-1), off.reshape(-1), cnt.reshape(-1),
      x1, w_gate, w_up, w_down, ln_g, ln_b)


def _moe_body(x1_ref, gates_ref, wg_ref, wu_ref, wd_ref, lg_ref, lb_ref, out_ref, xb_ref, *, alpha):
    e = pl.program_id(1)

    @pl.when(e == 0)
    def _():
        xb_ref[...] = x1_ref[...].astype(BF16)
        out_ref[...] = jnp.zeros(out_ref.shape, F32)

    y = _expert(xb_ref[...], wg_ref[0, 0], wu_ref[0, 0], wd_ref[0, 0])
    gates = gates_ref[...]
    lane = lax.broadcasted_iota(I32, gates.shape, 1)
    g = jnp.sum(jnp.where(lane == e, gates, 0.0), axis=1, keepdims=True)
    out_ref[...] += g * y

    @pl.when(e == N_EXPERTS - 1)
    def _():
        out_ref[...] = _layer_norm(alpha * x1_ref[...] + out_ref[...], lg_ref[0], lb_ref[0])


def _moe(x1, gates, w_gate, w_up, w_down, ln_g, ln_b, layer, tm, alpha):
    n = x1.shape[0]
    dff = w_gate.shape[-1]
    vec = pl.BlockSpec((1, 1, D_MODEL), lambda i, e: (layer, 0, 0))
    return pl.pallas_call(
        functools.partial(_moe_body, alpha=alpha),
        grid=(n // tm, N_EXPERTS),
        in_specs=[
            pl.BlockSpec((tm, D_MODEL), lambda i, e: (i, 0)),
            pl.BlockSpec((tm, LANES), lambda i, e: (i, 0)),
            pl.BlockSpec((1, 1, D_MODEL, dff), lambda i, e: (layer, e, 0, 0)),
            pl.BlockSpec((1, 1, D_MODEL, dff), lambda i, e: (layer, e, 0, 0)),
            pl.BlockSpec((1, 1, dff, D_MODEL), lambda i, e: (layer, e, 0, 0)),
            vec, vec,
        ],
        out_specs=pl.BlockSpec((tm, D_MODEL), lambda i, e: (i, 0)),
        out_shape=jax.ShapeDtypeStruct((n, D_MODEL), F32),
        scratch_shapes=[pltpu.VMEM((tm, D_MODEL), BF16)],
        compiler_params=_cparams(("arbitrary", "arbitrary")),
        name="moe",
    )(x1, gates, w_gate, w_up, w_down, ln_g, ln_b)


def _rope_tables(pos):
    half = RET_D // 2
    inv = ROPE_BASE ** (-jnp.arange(half, dtype=F32) / half)
    ang = pos.astype(F32)[:, None] * inv[None, :]
    cos, sin = jnp.cos(ang), jnp.sin(ang)
    return jnp.concatenate([cos, cos], axis=1), jnp.concatenate([-sin, sin], axis=1)


def kernel(x_prompt, x_sample, cache_k, cache_v, cache_kidx, state_ret, page_table, w_in, ret_gn_w, w_out,
           ln1_g, ln1_b, router_w, router_b, w_gate, w_up, w_down, ln2_g, ln2_b):
    depth = w_in.shape[0]
    B, T, D = x_prompt.shape
    DB, TS, _ = x_sample.shape
    n_pool = cache_k.shape[1]
    W = ATT_HEADS * ATT_HD
    past = page_table.shape[1] * PAGE
    alpha = (2 * depth) ** 0.25

    w_main = w_in[:, :, :MAIN_W].astype(BF16)
    w_tail = jnp.pad(w_in[:, :, MAIN_W:], ((0, 0), (0, 0), (0, TAIL_W - (w_in.shape[2] - MAIN_W)))).astype(BF16)
    cache_k_t = cache_k.transpose(0, 1, 3, 4, 2).reshape(depth, n_pool, W, PAGE)
    cache_v_t = cache_v.transpose(0, 1, 3, 4, 2).reshape(depth, n_pool, W, PAGE)
    cache_ki_t = cache_kidx.transpose(0, 1, 3, 2)
    rwt = router_w.T
    rb_col = router_b.reshape(N_EXPERTS, 1)
    vec3 = lambda a: a.reshape(depth, 1, a.shape[-1])
    gn3, l1g, l1b, l2g, l2b = vec3(ret_gn_w), vec3(ln1_g), vec3(ln1_b), vec3(ln2_g), vec3(ln2_b)
    cos_p, sin_p = _rope_tables(jnp.arange(T))
    cos_s, sin_s = _rope_tables(past + jnp.arange(TS))
    zero_state = jnp.zeros((B, RET_HEADS, RET_D, RET_D), F32)
    C = min(LANES, T)
    wg_b, wu_b, wd_b = w_gate.astype(BF16), w_up.astype(BF16), w_down.astype(BF16)

    def block(x2d, nb, nt, o_att_fn, cosf, sinf, state0, chunk, tm_proj, tm_merge, tm_moe, moe_fn, l):
        pm, pt = _project(x2d, w_main, w_tail, l, tm_proj)
        pm3 = pm.reshape(nb, nt, MAIN_W)
        pt3 = pt.reshape(nb, nt, TAIL_W)
        o_ret, st = _retention(pm3, cosf, sinf, state0, chunk)
        o_att = o_att_fn(pm3, pt3)
        x1, gates, route, counts = _merge(o_ret.reshape(nb * nt, -1), pm, o_att.reshape(nb * nt, -1), x2d, gn3,
                                          w_out, l1g, l1b, rwt, rb_col, l, tm_merge, alpha, tm_moe // tm_merge)
        routing = (route, counts) if moe_fn is _moe_routed else gates
        x2 = moe_fn(x1, routing, wg_b, wu_b, wd_b, l2g, l2b, l, tm_moe, alpha)
        k_a = pm3[:, :, 5 * W:6 * W].reshape(nb, nt, ATT_HEADS, ATT_HD)
        v_a = pm3[:, :, 6 * W:7 * W].reshape(nb, nt, ATT_HEADS, ATT_HD)
        k_i = pt3[:, :, :IDX_DIM]
        return x2, k_a, v_a, k_i, st

    xp = x_prompt.reshape(B * T, D)
    xs = x_sample.reshape(DB * TS, D)
    outs_p, outs_s = [], []
    for l in range(depth):
        xp, *rest = block(xp, B, T, _dsa_prompt, cos_p, sin_p, zero_state, C,
                          min(2048, B * T), min(512, B * T), min(2048, B * T), _moe_routed, l)
        outs_p.append(rest)
        sample_att = functools.partial(_dsa_sample, cache_k_t=cache_k_t, cache_v_t=cache_v_t,
                                       cache_ki_t=cache_ki_t, page_table=page_table, layer=l)
        xs, *rest = block(xs, DB, TS, sample_att, cos_s, sin_s, state_ret[l], TS,
                          DB * TS, DB * TS, DB * TS, _moe, l)
        outs_s.append(rest)
    stack = lambda outs, i: jnp.stack([o[i] for o in outs])
    return (xp.reshape(B, T, D), xs.reshape(DB, TS, D),
            stack(outs_p, 0), stack(outs_p, 1), stack(outs_p, 2), stack(outs_p, 3),
            stack(outs_s, 0), stack(outs_s, 1), stack(outs_s, 2), stack(outs_s, 3))
```
